```python
import math
import jax
import jax.numpy as jnp
from jax import lax
import numpy as np

D_MODEL = 1024
BATCH = 2
SEQ = 8192
DEPTH = 4

CHUNK = 64
N_EVEN = (DEPTH + 1) // 2
N_ODD = DEPTH // 2
MIX_W = D_MODEL
A_HEADS = 8
A_HD = 64
A_W = A_HEADS * A_HD
A_DECAY_R = 32
A_AAA_R = 32
A_MV_R = 32
A_GATE_R = 96
A_COLS = 3 * A_W + A_DECAY_R + A_AAA_R + A_GATE_R
A_SPLITS = (A_W, 2 * A_W, 3 * A_W, 3 * A_W + A_DECAY_R, 3 * A_W + A_DECAY_R + A_AAA_R)
RWKV_LN_EPS = 64e-5
B_W = MIX_W - A_W
B_GROUPS = 8
B_GD = B_W // B_GROUPS
SGU_BLOCK = 128
SGU_LN_EPS = 1e-5
EVEN_COLS = A_COLS + 2 * B_W
C_HEADS = 8
C_HD = 64
C_W = C_HEADS * C_HD
D_HEADS = 4
D_HD = 128
D_W = D_HEADS * D_HD
D_CONV = 4
ODD_COLS = 4 * C_W + 4 * D_W + 2 * D_HEADS
D_FF = 2816
FFN_CONV = 3
NORM_EPS = 1e-6

kernel_name = 'hybrid_rwkv7_gmlp_hgrn2_gdn_trunk'


def rms_norm(x, g, eps=NORM_EPS):
    xf = x.astype(jnp.float32)
    y = xf * lax.rsqrt(jnp.mean(xf * xf, axis=-1, keepdims=True) + eps)
    return (y * g.astype(jnp.float32)).astype(x.dtype)


def group_layer_norm(x, g, b, n_groups, eps):
    shp = x.shape
    xf = x.astype(jnp.float32).reshape(shp[:-1] + (n_groups, shp[-1] // n_groups))
    mu = jnp.mean(xf, axis=-1, keepdims=True)
    var = jnp.mean(jnp.square(xf - mu), axis=-1, keepdims=True)
    y = ((xf - mu) * lax.rsqrt(var + eps)).reshape(shp)
    return (y * g + b).astype(x.dtype)


def l2_normalize(x, eps=1e-6):
    xf = x.astype(jnp.float32)
    return xf * lax.rsqrt(jnp.sum(xf * xf, axis=-1, keepdims=True) + eps)


def token_shift(x):
    return jnp.pad(x, ((0, 0), (1, 0), (0, 0)))[:, :-1]


def causal_dwconv(x, w):
    k_w = w.shape[0]
    return lax.conv_general_dilated(
        x, w[:, None, :].astype(x.dtype), window_strides=(1,), padding=[(k_w - 1, 0)],
        dimension_numbers=('NWC', 'WIO', 'NWC'), feature_group_count=x.shape[-1])


def to_heads(x, n_heads):
    return x.reshape(x.shape[:-1] + (n_heads, x.shape[-1] // n_heads))


def split_heads(x, n_heads):
    return to_heads(x, n_heads).transpose(0, 2, 1, 3)


def merge_heads(x):
    b, h, t, d = x.shape
    return x.transpose(0, 2, 1, 3).reshape(b, t, h * d)


def rwkv7_mix(za, v_first, vres, mu, w0, w_up, a0, a_up, g_up, k_k, k_a, r_k, ln_g, ln_b):
    f32 = jnp.float32
    bsz = za.shape[0]
    za = za + mu * (token_shift(za) - za)
    r, k, v, xw, xa, xg = jnp.split(za, A_SPLITS, axis=-1)
    if vres is not None:
        v_lr, v_up, v0 = vres
        v = v + (v_first - v) * jax.nn.sigmoid(v0 + v_lr @ v_up)
    w_log = -jax.nn.softplus(-(w0 + jnp.tanh(xw) @ w_up).astype(f32)) - 0.5
    decay = jnp.exp(-jnp.exp(w_log))
    a = jax.nn.sigmoid((a0 + xa @ a_up).astype(f32))
    g = jax.nn.sigmoid(xg) @ g_up
    kk = l2_normalize(to_heads(k * k_k, A_HEADS))
    k_mod = k.astype(f32) * (1.0 + (a - 1.0) * k_a)
    rh, kh, vh, wh, ah = (to_heads(t.astype(f32), A_HEADS) for t in (r, k_mod, v, decay, a))

    def step(s, inp):
        r_t, w_t, k_t, v_t, kk_t, a_t = inp
        sa = jnp.einsum('bhvk,bhk->bhv', s, kk_t)
        s = (s * w_t[:, :, None, :] - sa[..., None] * (kk_t * a_t)[:, :, None, :]
             + v_t[..., None] * k_t[:, :, None, :])
        return s, jnp.einsum('bhvk,bhk->bhv', s, r_t)

    s0 = jnp.zeros((bsz, A_HEADS, A_HD, A_HD), f32)
    xs = tuple(jnp.moveaxis(t, 1, 0) for t in (rh, wh, kh, vh, kk, ah))
    _, y = lax.scan(step, s0, xs)
    y = jnp.moveaxis(y, 0, 1)
    mu_y = jnp.mean(y, axis=-1, keepdims=True)
    var_y = jnp.mean(jnp.square(y - mu_y), axis=-1, keepdims=True)
    y = ((y - mu_y) * lax.rsqrt(var_y + RWKV_LN_EPS) * ln_g.reshape(A_HEADS, A_HD)
         + ln_b.reshape(A_HEADS, A_HD))
    y = y + jnp.sum(rh * kh * r_k, axis=-1, keepdims=True) * vh
    y = y.reshape(za.shape[:2] + (A_W,)).astype(za.dtype) * g
    return y, v


def sgu_mix(zb, ln_g, ln_b, w_s, b_s):
    bsz, t_len, _ = zb.shape
    u, v = jnp.split(jax.nn.gelu(zb, approximate=False), 2, axis=-1)
    v = group_layer_norm(v, ln_g, ln_b, B_GROUPS, SGU_LN_EPS)
    v = v.reshape(bsz, t_len // SGU_BLOCK, SGU_BLOCK, B_GROUPS, B_GD)
    pos_chunk = jnp.arange(SGU_BLOCK) // CHUNK
    mask = pos_chunk[:, None] >= pos_chunk[None, :]
    w = jnp.where(mask, w_s, 0.0).astype(v.dtype)
    mixed = jnp.einsum('gij,bnjgc->bnigc', w, v) + b_s.T[None, None, :, :, None]
    return u * mixed.reshape(bsz, t_len, B_W)


def gla_chunked(q, k, v, log_f):
    bsz, h, t_len, dk = q.shape
    dv = v.shape[-1]
    n = t_len // CHUNK
    q, k, v, log_f = (t.astype(jnp.float32).reshape(bsz, h, n, CHUNK, t.shape[-1])
                      for t in (q, k, v, log_f))
    b = jnp.cumsum(log_f, axis=3)
    b_ref = b[:, :, :, CHUNK // 2:CHUNK // 2 + 1]
    b_last = b[:, :, :, -1:]
    causal = jnp.tril(jnp.ones((CHUNK, CHUNK), dtype=bool))
    scores = jnp.einsum('bhntc,bhnsc->bhnts', q * jnp.exp(b - b_ref), k * jnp.exp(b_ref - b))
    o_intra = jnp.einsum('bhnts,bhnsv->bhntv', jnp.where(causal, scores, 0.0), v)
    q_in = q * jnp.exp(b)
    k_st = k * jnp.exp(b_last - b)
    decay_last = jnp.exp(b_last[:, :, :, 0])

    def step(s, inp):
        q_c, k_c, v_c, d_c = inp
        o_c = jnp.einsum('bhtk,bhkv->bhtv', q_c, s)
        s = s * d_c[..., None] + jnp.einsum('bhtk,bhtv->bhkv', k_c, v_c)
        return s, o_c

    s0 = jnp.zeros((bsz, h, dk, dv), jnp.float32)
    xs = tuple(jnp.moveaxis(t, 2, 0) for t in (q_in, k_st, v, decay_last))
    _, o_inter = lax.scan(step, s0, xs)
    o = jnp.moveaxis(o_inter, 0, 2) + o_intra
    return o.reshape(bsz, h, t_len, dv)


def gated_delta_chunked(q, k, v, g, beta):
    bsz, h, t_len, dk = q.shape
    dv = v.shape[-1]
    n = t_len // CHUNK
    q, k, v = (t.astype(jnp.float32).reshape(bsz, h, n, CHUNK, t.shape[-1]) for t in (q, k, v))
    g, beta = (t.astype(jnp.float32).reshape(bsz, h, n, CHUNK) for t in (g, beta))
    gc = jnp.cumsum(g, axis=-1)
    causal = jnp.tril(jnp.ones((CHUNK, CHUNK), dtype=bool))
    strict = jnp.tril(jnp.ones((CHUNK, CHUNK), dtype=bool), -1)
    diff = gc[..., :, None] - gc[..., None, :]
    decay = jnp.where(causal, jnp.exp(jnp.where(causal, diff, 0.0)), 0.0)
    kb = k * beta[..., None]
    lower = jnp.where(strict, jnp.einsum('bhntc,bhnsc->bhnts', kb, k) * decay, 0.0)
    rhs = jnp.concatenate([v * beta[..., None], kb * jnp.exp(gc)[..., None]], axis=-1)
    sol = lax.linalg.triangular_solve(lower + jnp.eye(CHUNK, dtype=jnp.float32), rhs,
                                      left_side=True, lower=True)
    u, w = sol[..., :dv], sol[..., dv:]
    qk = jnp.where(causal, jnp.einsum('bhntc,bhnsc->bhnts', q, k) * decay, 0.0)
    q_in = q * jnp.exp(gc)[..., None]
    k_st = k * jnp.exp(gc[..., -1:] - gc)[..., None]
    decay_last = jnp.exp(gc[..., -1])

    def step(s, inp):
        u_c, w_c, q_c, k_c, qk_c, d_c = inp
        v_new = u_c - jnp.einsum('bhtk,bhkv->bhtv', w_c, s)
        o_c = jnp.einsum('bhtk,bhkv->bhtv', q_c, s) + jnp.einsum('bhts,bhsv->bhtv', qk_c, v_new)
        s = s * d_c[..., None, None] + jnp.einsum('bhtk,bhtv->bhkv', k_c, v_new)
        return s, o_c

    s0 = jnp.zeros((bsz, h, dk, dv), jnp.float32)
    xs = tuple(jnp.moveaxis(t, 2, 0) for t in (u, w, q_in, k_st, qk, decay_last))
    _, o = lax.scan(step, s0, xs)
    return jnp.moveaxis(o, 0, 2).reshape(bsz, h, t_len, dv)


def hgrn2_mix(zc, lb, norm_g):
    q, f_logit, i, gate = jnp.split(zc, 4, axis=-1)
    f = lb + (1.0 - lb) * jax.nn.sigmoid(f_logit.astype(jnp.float32))
    o = gla_chunked(split_heads(jax.nn.silu(q), C_HEADS), split_heads(1.0 - f, C_HEADS),
                    split_heads(i, C_HEADS), split_heads(jnp.log(f), C_HEADS))
    o = rms_norm(o, norm_g.reshape(C_HEADS, 1, C_HD))
    return merge_heads(o).astype(zc.dtype) * jax.nn.silu(gate)


def gdn_mix(zd, conv_w, a_log, dt_bias, norm_g):
    qkv, z, b, a = jnp.split(zd, [3 * D_W, 4 * D_W, 4 * D_W + D_HEADS], axis=-1)
    qkv = jax.nn.silu(causal_dwconv(qkv, conv_w))
    q, k, v = jnp.split(qkv, 3, axis=-1)
    q = l2_normalize(split_heads(q, D_HEADS)) * (D_HD ** -0.5)
    k = l2_normalize(split_heads(k, D_HEADS))
    v = split_heads(v, D_HEADS)
    beta = jax.nn.sigmoid(b.astype(jnp.float32)).transpose(0, 2, 1)
    g = (-jnp.exp(a_log.astype(jnp.float32))
         * jax.nn.softplus(a.astype(jnp.float32) + dt_bias)).transpose(0, 2, 1)
    o = gated_delta_chunked(q, k, v, g, beta)
    o = rms_norm(o, norm_g)
    return merge_heads(o).astype(zd.dtype) * jax.nn.silu(z)


def conv_glu_ffn(h, w_in, conv_w, conv_b, w_out):
    gate, up = jnp.split(h @ w_in, 2, axis=-1)
    gate = causal_dwconv(gate, conv_w) + conv_b
    return (jax.nn.gelu(gate, approximate=True) * up) @ w_out


def setup_inputs(seed: int = 0) -> dict:
    key = jax.random.key(seed)
    keys = jax.random.split(key, 48)
    ctr = iter(range(48))

    def nk():
        return keys[next(ctr)]

    def nrm(shape, scale):
        return jax.random.normal(nk(), shape, jnp.float32) * scale

    def gain(shape):
        return 1.0 + nrm(shape, 0.02)

    d = D_MODEL
    dt = jnp.exp(jax.random.uniform(nk(), (N_ODD, D_HEADS), jnp.float32,
                                    minval=math.log(1e-3), maxval=math.log(1e-1)))
    return {
        'x': nrm((BATCH, SEQ, d), 1.0),
        'norm_mix_pre': gain((DEPTH, d)),
        'norm_mix_post': gain((DEPTH, d)),
        'norm_ffn_pre': gain((DEPTH, d)),
        'norm_ffn_post': gain((DEPTH, d)),
        'ev_w_in': nrm((N_EVEN, d, EVEN_COLS), d ** -0.5),
        'ev_w_out': nrm((N_EVEN, MIX_W, d), MIX_W ** -0.5),
        'rwkv_mu': jax.random.uniform(nk(), (N_EVEN, A_COLS), jnp.float32, minval=0.2, maxval=0.8),
        'rwkv_w0': jnp.linspace(-6.5, -1.5, A_W, dtype=jnp.float32)[None, :] + nrm((N_EVEN, A_W), 0.1),
        'rwkv_w_up': nrm((N_EVEN, A_DECAY_R, A_W), 0.5 * A_DECAY_R ** -0.5),
        'rwkv_a0': nrm((N_EVEN, A_W), 0.1),
        'rwkv_a_up': nrm((N_EVEN, A_AAA_R, A_W), 0.5 * A_AAA_R ** -0.5),
        'rwkv_g_up': nrm((N_EVEN, A_GATE_R, A_W), A_GATE_R ** -0.5),
        'rwkv_k_k': 0.85 + nrm((N_EVEN, A_W), 0.02),
        'rwkv_k_a': gain((N_EVEN, A_W)),
        'rwkv_r_k': nrm((N_EVEN, A_HEADS, A_HD), 0.1),
        'rwkv_ln_g': gain((N_EVEN, A_W)),
        'rwkv_ln_b': nrm((N_EVEN, A_W), 0.02),
        'rwkv_vres_down': nrm((N_EVEN - 1, d, A_MV_R), d ** -0.5),
        'rwkv_vres_up': nrm((N_EVEN - 1, A_MV_R, A_W), 0.5 * A_MV_R ** -0.5),
        'rwkv_v0': gain((N_EVEN - 1, A_W)),
        'sgu_ln_g': gain((N_EVEN, B_W)),
        'sgu_ln_b': nrm((N_EVEN, B_W), 0.02),
        'sgu_w': nrm((N_EVEN, B_GROUPS, SGU_BLOCK, SGU_BLOCK), SGU_BLOCK ** -0.5),
        'sgu_b': gain((N_EVEN, B_GROUPS, SGU_BLOCK)),
        'od_w_in': nrm((N_ODD, d, ODD_COLS), d ** -0.5),
        'od_w_out': nrm((N_ODD, MIX_W, d), MIX_W ** -0.5),
        'hgrn_lb_logits': nrm((N_ODD, C_W), 0.5),
        'hgrn_norm_g': gain((N_ODD, C_W)),
        'gdn_conv_w': nrm((N_ODD, D_CONV, 3 * D_W), D_CONV ** -0.5),
        'gdn_a_log': jnp.log(jax.random.uniform(nk(), (N_ODD, D_HEADS), jnp.float32, minval=1.0, maxval=16.0)),
        'gdn_dt_bias': dt + jnp.log(-jnp.expm1(-dt)),
        'gdn_norm_g': gain((N_ODD, D_HD)),
        'ffn_w_in': nrm((DEPTH, d, 2 * D_FF), d ** -0.5),
        'ffn_conv_w': nrm((DEPTH, FFN_CONV, D_FF), FFN_CONV ** -0.5),
        'ffn_conv_b': nrm((DEPTH, D_FF), 0.02),
        'ffn_w_out': nrm((DEPTH, D_FF, d), D_FF ** -0.5),
    }


def reference(x, norm_mix_pre, norm_mix_post, norm_ffn_pre, norm_ffn_post,
              ev_w_in, ev_w_out, rwkv_mu, rwkv_w0, rwkv_w_up, rwkv_a0, rwkv_a_up, rwkv_g_up,
              rwkv_k_k, rwkv_k_a, rwkv_r_k, rwkv_ln_g, rwkv_ln_b,
              rwkv_vres_down, rwkv_vres_up, rwkv_v0,
              sgu_ln_g, sgu_ln_b, sgu_w, sgu_b,
              od_w_in, od_w_out, hgrn_lb_logits, hgrn_norm_g,
              gdn_conv_w, gdn_a_log, gdn_dt_bias, gdn_norm_g,
              ffn_w_in, ffn_conv_w, ffn_conv_b, ffn_w_out):
    p = jax.nn.softmax(hgrn_lb_logits.astype(jnp.float32), axis=0)
    lower_bounds = jnp.cumsum(p, axis=0) - p[0]
    v_first = None
    for l in range(DEPTH):
        h = rms_norm(x, norm_mix_pre[l])
        if l % 2 == 0:
            e = l // 2
            z = h @ ev_w_in[e]
            za, zb = z[..., :A_COLS], z[..., A_COLS:]
            vres = None if e == 0 else (h @ rwkv_vres_down[e - 1], rwkv_vres_up[e - 1], rwkv_v0[e - 1])
            ya, v_a = rwkv7_mix(za, v_first, vres, rwkv_mu[e], rwkv_w0[e], rwkv_w_up[e], rwkv_a0[e],
                                rwkv_a_up[e], rwkv_g_up[e], rwkv_k_k[e], rwkv_k_a[e], rwkv_r_k[e],
                                rwkv_ln_g[e], rwkv_ln_b[e])
            if e == 0:
                v_first = v_a
            yb = sgu_mix(zb, sgu_ln_g[e], sgu_ln_b[e], sgu_w[e], sgu_b[e])
            mix = jnp.concatenate([ya, yb], axis=-1) @ ev_w_out[e]
        else:
            o = l // 2
            z = h @ od_w_in[o]
            zc, zd = z[..., :4 * C_W], z[..., 4 * C_W:]
            yc = hgrn2_mix(zc, lower_bounds[o], hgrn_norm_g[o])
            yd = gdn_mix(zd, gdn_conv_w[o], gdn_a_log[o], gdn_dt_bias[o], gdn_norm_g[o])
            mix = jnp.concatenate([yc, yd], axis=-1) @ od_w_out[o]
        x = x + rms_norm(mix, norm_mix_post[l])
        h = rms_norm(x, norm_ffn_pre[l])
        x = x + rms_norm(conv_glu_ffn(h, ffn_w_in[l], ffn_conv_w[l], ffn_conv_b[l], ffn_w_out[l]),
                         norm_ffn_post[l])
    return x
```

```python
import functools

import jax
import jax.numpy as jnp
from jax import lax
from jax.experimental import pallas as pl
from jax.experimental.pallas import tpu as pltpu

F32 = jnp.float32
BF16 = jnp.bfloat16

D_MODEL = 1024
CHUNK = 64
HALF_W = 512
A_DECAY_R = 32
A_AAA_R = 32
A_MV_R = 32
A_GATE_R = 96
A_LR_W = 256
SGU_BLOCK = 128
B_GROUPS = 8
D_HEADS = 4
D_HD = 128
D_CONV = 4
D_FF = 2816
NORM_EPS = 1e-6
RWKV_LN_EPS = 64e-5
SGU_LN_EPS = 1e-5
L2_EPS = 1e-6

ROWS = 128
HALO = 8
LANES = 128
VMEM_LIMIT = 56 * 1024 * 1024


def _dot(a, b):
    return jnp.dot(a, b, preferred_element_type=F32)


def _dot_nt(a, b):
    return lax.dot_general(a, b, (((1,), (1,)), ((), ())), preferred_element_type=F32)


def _dot_tn(a, b):
    return lax.dot_general(a, b, (((0,), (0,)), ((), ())), preferred_element_type=F32)


def _dot_exact_lhs(m_bf16, x):
    hi = x.astype(BF16)
    lo = (x - hi.astype(F32)).astype(BF16)
    return _dot(m_bf16, hi) + _dot(m_bf16, lo)


def _dot_exact_rhs(x, m_bf16):
    hi = x.astype(BF16)
    lo = (x - hi.astype(F32)).astype(BF16)
    return _dot(hi, m_bf16) + _dot(lo, m_bf16)


def _sigmoid(x):
    return 1.0 / (1.0 + jnp.exp(-x))


def _softplus(x):
    return jnp.maximum(x, 0.0) + jnp.log(1.0 + jnp.exp(-jnp.abs(x)))


def _rms(x, g):
    return x * lax.rsqrt(jnp.mean(x * x, axis=-1, keepdims=True) + NORM_EPS) * g


def _iota(shape, dim):
    return lax.broadcasted_iota(jnp.int32, shape, dim)


def _chunk_tri(n, strict=False):
    r = _iota((n, n), 0)
    c = _iota((n, n), 1)
    same = (r // CHUNK) == (c // CHUNK)
    return same & ((c < r) if strict else (c <= r))


def _seg_ones(width, seg):
    r = _iota((width, width), 0)
    c = _iota((width, width), 1)
    return jnp.where((r // seg) == (c // seg), 1.0, 0.0).astype(BF16)


def _stack_heads(x):
    lane = _iota((CHUNK, LANES), 1)
    m1 = lane < 64
    top, bot = x[0:CHUNK], x[CHUNK:2 * CHUNK]
    return jnp.concatenate([jnp.where(m1, top, 0.0), jnp.where(m1, 0.0, top),
                            jnp.where(m1, bot, 0.0), jnp.where(m1, 0.0, bot)], axis=0)


def _unstack_heads(o):
    return jnp.concatenate([o[0:64] + o[64:128], o[128:192] + o[192:256]], axis=0)


def _neumann_inverse(n_mat):
    size = n_mat.shape[0]
    eye = jnp.where(_iota((size, size), 0) == _iota((size, size), 1), 1.0, 0.0)
    p = eye + n_mat
    m = n_mat
    for _ in range(5):
        m = _dot(m, m)
        p = p + _dot(p, m)
    return p


def _chunk_rows(x, r0, r1):
    w = x.shape[1]
    return jnp.concatenate([jnp.broadcast_to(x[r0:r0 + 1], (CHUNK, w)),
                            jnp.broadcast_to(x[r1:r1 + 1], (CHUNK, w))], axis=0)


def _with_halo(x, halo, keep):
    return jnp.concatenate([jnp.where(keep, halo, 0.0), x], axis=0)


def _shifted(ext, s):
    if s == 0:
        return ext[HALO:]
    return pltpu.roll(ext, s, 0)[HALO:]


def _params(sem):
    return pltpu.CompilerParams(dimension_semantics=sem, vmem_limit_bytes=VMEM_LIMIT)


def _row_spec(tm, width, col=0):
    return pl.BlockSpec((tm, width), lambda i, col=col: (i, col))


def _halo_spec(tm, width, col=0):
    step = tm // HALO
    return pl.BlockSpec((HALO, width), lambda i, col=col: (jnp.maximum(i * step - 1, 0), col))


def _const_spec(shape):
    return pl.BlockSpec(shape, lambda *_: (0,) * len(shape))


def _seq_spec(width, nblk, col=0):
    return pl.BlockSpec((ROWS, width), lambda b, j, col=col: (b * nblk + j, col))


def _inproj_body(x_ref, g_ref, w_ref, *o_refs, widths):
    h = _rms(x_ref[...], g_ref[...]).astype(BF16)
    off = 0
    for o_ref, wd in zip(o_refs, widths):
        o_ref[...] = _dot(h, w_ref[:, off:off + wd])
        off += wd


def _inproj(x, g, w, widths, tm=256):
    n, d = x.shape
    wt = w.shape[1]
    return pl.pallas_call(
        functools.partial(_inproj_body, widths=widths),
        grid=(n // tm,),
        in_specs=[_row_spec(tm, d), _const_spec((1, d)), _const_spec((d, wt))],
        out_specs=[_row_spec(tm, wd) for wd in widths],
        out_shape=[jax.ShapeDtypeStruct((n, wd), F32) for wd in widths],
        compiler_params=_params(("parallel",)),
        name="inproj",
    )(x, g, w)


def _rwkv_prep_body(*refs, has_vres, tiles_per_batch):
    if has_vres:
        (z_ref, zh_ref, l_ref, lh_ref, vf_ref, mu_ref, mul_ref, w0_ref, wup_ref, a0_ref, aup_ref, gup_ref,
         kk_ref, ka_ref, v0_ref, vup_ref,
         r_o, lw_o, k_o, v_o, kk_o, a_o, g_o) = refs
    else:
        (z_ref, zh_ref, l_ref, lh_ref, mu_ref, mul_ref, w0_ref, wup_ref, a0_ref, aup_ref, gup_ref,
         kk_ref, ka_ref,
         r_o, lw_o, k_o, v_o, kk_o, a_o, g_o) = refs
    keep = (pl.program_id(0) % tiles_per_batch) != 0
    z = z_ref[...]
    zprev = _shifted(_with_halo(z, zh_ref[...], keep), 1)
    za = z + mu_ref[...] * (zprev - z)
    zl = l_ref[...]
    zlprev = _shifted(_with_halo(zl, lh_ref[...], keep), 1)
    zl = zl + mul_ref[...] * (zlprev - zl)

    r = za[:, 0:HALF_W]
    k = za[:, HALF_W:2 * HALF_W]
    v = za[:, 2 * HALF_W:3 * HALF_W]
    if has_vres:
        gate = _sigmoid(v0_ref[...] + _dot(zl, vup_ref[...]))
        v = v + (vf_ref[...] - v) * gate
    w_log = -_softplus(-(w0_ref[...] + _dot(jnp.tanh(zl), wup_ref[...]))) - 0.5
    lw = -jnp.exp(w_log)
    a = _sigmoid(a0_ref[...] + _dot(zl, aup_ref[...]))
    g = _dot(_sigmoid(zl), gup_ref[...])
    kkk = k * kk_ref[...]
    ss = _dot_exact_rhs(kkk * kkk, _seg_ones(HALF_W, 64))
    kk = kkk * lax.rsqrt(ss + L2_EPS)
    kmod = k * (1.0 + (a - 1.0) * ka_ref[...])
    r_o[...] = r
    lw_o[...] = lw
    k_o[...] = kmod
    v_o[...] = v
    kk_o[...] = kk
    a_o[...] = a
    g_o[...] = g


def _rwkv_prep(z_rkv, z_lr, v_first, prm, seq_len, tm=256):
    n = z_rkv.shape[0]
    has_vres = v_first is not None
    w3 = 3 * HALF_W
    ins = [z_rkv, z_rkv, z_lr, z_lr]
    specs = [_row_spec(tm, w3), _halo_spec(tm, w3), _row_spec(tm, A_LR_W), _halo_spec(tm, A_LR_W)]
    if has_vres:
        ins.append(v_first)
        specs.append(_row_spec(tm, HALF_W))
    names = ["mu", "mul", "w0", "wup", "a0", "aup", "gup", "k_k", "k_a"] + (["v0", "vup"] if has_vres else [])
    for nm in names:
        ins.append(prm[nm])
        specs.append(_const_spec(prm[nm].shape))
    return pl.pallas_call(
        functools.partial(_rwkv_prep_body, has_vres=has_vres, tiles_per_batch=seq_len // tm),
        grid=(n // tm,),
        in_specs=specs,
        out_specs=[_row_spec(tm, HALF_W)] * 7,
        out_shape=[jax.ShapeDtypeStruct((n, HALF_W), F32)] * 7,
        compiler_params=_params(("parallel",)),
        name="rwkv_prep",
    )(*ins)


def _rwkv_rec_body(r_ref, lw_ref, k_ref, v_ref, kk_ref, a_ref, y_ref, s_ref):
    @pl.when(pl.program_id(1) == 0)
    def _():
        s_ref[...] = jnp.zeros_like(s_ref)

    tri = jnp.where(_chunk_tri(ROWS), 1.0, 0.0).astype(BF16)
    strict = _chunk_tri(2 * ROWS, strict=True)
    incl = _chunk_tri(2 * ROWS)
    lw = lw_ref[...]
    b = _dot_exact_lhs(tri, lw)
    bref = _chunk_rows(b, CHUNK // 2, CHUNK + CHUNK // 2)
    blast = _chunk_rows(b, CHUNK - 1, 2 * CHUNK - 1)
    e_pos = jnp.exp(b - bref)
    e_neg = jnp.exp(bref - b)
    e_in = jnp.exp(b)
    e_st = jnp.exp(blast - b)
    e_exc = jnp.exp(-lw)
    e_last = jnp.exp(blast)
    kk = kk_ref[...]
    kka = kk * a_ref[...]
    r = r_ref[...]
    k = k_ref[...]
    neg_kk_exc = -(kk * e_exc)
    a_t = neg_kk_exc * e_pos
    a_in = neg_kk_exc * e_in
    r_t = r * e_pos
    r_in = r * e_in
    b_t = kka * e_neg
    b_st = kka * e_st
    k_t = k * e_neg
    k_st = k * e_st
    v = v_ref[...]

    for p in range(HALF_W // LANES):
        sl = slice(p * LANES, (p + 1) * LANES)
        lhs = jnp.concatenate([_stack_heads(a_t[:, sl]), _stack_heads(r_t[:, sl])], axis=0)
        rhs = jnp.concatenate([_stack_heads(b_t[:, sl]), _stack_heads(k_t[:, sl])], axis=0)
        q = _dot_nt(lhs, rhs)
        a_ab = jnp.where(strict, q[0:256, 0:256], 0.0)
        a_ak = jnp.where(strict, q[0:256, 256:512], 0.0)
        a_rb = jnp.where(incl, q[256:512, 0:256], 0.0)
        a_rk = jnp.where(incl, q[256:512, 256:512], 0.0)
        t_inv = _neumann_inverse(a_ab)
        v_s = _stack_heads(v[:, sl])
        av = _dot(jnp.concatenate([a_ak, a_rk], axis=0), v_s)
        tt = _dot(t_inv, jnp.concatenate([_stack_heads(a_in[:, sl]), av[0:256]], axis=1))
        w_a, u0 = tt[:, 0:LANES], tt[:, LANES:2 * LANES]
        o0 = av[256:512]
        r_in_s = _stack_heads(r_in[:, sl])
        b_st_s = _stack_heads(b_st[:, sl])
        k_st_s = _stack_heads(k_st[:, sl])
        state = s_ref[p]
        outs = []
        for c in range(2):
            rs = slice(c * LANES, (c + 1) * LANES)
            x = _dot_nt(jnp.concatenate([w_a[rs], r_in_s[rs]], axis=0), state)
            u = u0[rs] + x[0:LANES]
            o_s = x[LANES:2 * LANES] + _dot(a_rb[rs, rs], u) + o0[rs]
            upd = _dot_tn(jnp.concatenate([u, v_s[rs]], axis=0),
                          jnp.concatenate([b_st_s[rs], k_st_s[rs]], axis=0))
            state = state * e_last[c * CHUNK:c * CHUNK + 1, sl] + upd
            outs.append(o_s[0:CHUNK] + o_s[CHUNK:2 * CHUNK])
        s_ref[p] = state
        y_ref[:, sl] = jnp.concatenate(outs, axis=0)


def _rwkv_rec(r, lw, k, v, kk, a, batch, seq_len):
    n = r.shape[0]
    nblk = seq_len // ROWS
    spec = _seq_spec(HALF_W, nblk)
    return pl.pallas_call(
        _rwkv_rec_body,
        grid=(batch, nblk),
        in_specs=[spec] * 6,
        out_specs=spec,
        out_shape=jax.ShapeDtypeStruct((n, HALF_W), F32),
        scratch_shapes=[pltpu.VMEM((HALF_W // LANES, LANES, LANES), F32)],
        compiler_params=_params(("parallel", "arbitrary")),
        name="rwkv_rec",
    )(r, lw, k, v, kk, a)


def _rwkv_post_body(y_ref, r_ref, k_ref, v_ref, g_ref, lng_ref, lnb_ref, rk_ref, o_ref):
    seg = _seg_ones(HALF_W, 64)
    y = y_ref[...]
    mu = _dot_exact_rhs(y, seg) * (1.0 / 64)
    d = y - mu
    var = _dot_exact_rhs(d * d, seg) * (1.0 / 64)
    yn = d * lax.rsqrt(var + RWKV_LN_EPS) * lng_ref[...] + lnb_ref[...]
    bonus = _dot_exact_rhs(r_ref[...] * k_ref[...] * rk_ref[...], seg)
    o_ref[...] = (yn + bonus * v_ref[...]) * g_ref[...]


def _rwkv_post(y, r, k, v, g, ln_g, ln_b, r_k, tm=512):
    n = y.shape[0]
    return pl.pallas_call(
        _rwkv_post_body,
        grid=(n // tm,),
        in_specs=[_row_spec(tm, HALF_W)] * 5 + [_const_spec((1, HALF_W))] * 3,
        out_specs=_row_spec(tm, HALF_W),
        out_shape=jax.ShapeDtypeStruct((n, HALF_W), F32),
        compiler_params=_params(("parallel",)),
        name="rwkv_post",
    )(y, r, k, v, g, ln_g, ln_b, r_k)


def _gelu_erf(x):
    return 0.5 * x * (1.0 + lax.erf(x * (2.0 ** -0.5)))


def _sgu_body(u_ref, v_ref, lng_ref, lnb_ref, w_ref, bias_ref, o_ref, *, tm):
    seg = _seg_ones(HALF_W, 64)
    u = _gelu_erf(u_ref[...])
    v = _gelu_erf(v_ref[...])
    mu = _dot_exact_rhs(v, seg) * (1.0 / 64)
    d = v - mu
    var = _dot_exact_rhs(d * d, seg) * (1.0 / 64)
    vn = d * lax.rsqrt(var + SGU_LN_EPS) * lng_ref[...] + lnb_ref[...]
    r = _iota((SGU_BLOCK, 2 * SGU_BLOCK), 0)
    c = _iota((SGU_BLOCK, 2 * SGU_BLOCK), 1)
    causal = (r // CHUNK) >= ((c % SGU_BLOCK) // CHUNK)
    lane = _iota((SGU_BLOCK, LANES), 1)
    m1 = lane < 64
    bias = bias_ref[...]
    for p in range(HALF_W // LANES):
        sl = slice(p * LANES, (p + 1) * LANES)
        w = jnp.where(causal, w_ref[p], 0.0)
        for nb in range(tm // SGU_BLOCK):
            rs = slice(nb * SGU_BLOCK, (nb + 1) * SGU_BLOCK)
            blk = vn[rs, sl]
            stacked = jnp.concatenate([jnp.where(m1, blk, 0.0), jnp.where(m1, 0.0, blk)], axis=0)
            mixed = _dot(w, stacked) + bias[:, sl]
            o_ref[rs, sl] = u[rs, sl] * mixed


def _sgu(z_b, ln_g, ln_b, w_pairs, bias, tm=256):
    n = z_b.shape[0]
    return pl.pallas_call(
        functools.partial(_sgu_body, tm=tm),
        grid=(n // tm,),
        in_specs=[_row_spec(tm, HALF_W, 0), _row_spec(tm, HALF_W, 1),
                  _const_spec((1, HALF_W)), _const_spec((1, HALF_W)),
                  _const_spec(w_pairs.shape), _const_spec(bias.shape)],
        out_specs=_row_spec(tm, HALF_W),
        out_shape=jax.ShapeDtypeStruct((n, HALF_W), F32),
        compiler_params=_params(("parallel",)),
        name="sgu",
    )(z_b, z_b, ln_g, ln_b, w_pairs, bias)


def _hgrn_body(q_ref, f_ref, i_ref, gt_ref, lbl_ref, ng_ref, y_ref, s_ref, *, layer):
    @pl.when(pl.program_id(1) == 0)
    def _():
        s_ref[...] = jnp.zeros_like(s_ref)

    logits = lbl_ref[...]
    e = jnp.exp(logits - jnp.max(logits, axis=0, keepdims=True))
    prob = e / jnp.sum(e, axis=0, keepdims=True)
    lb = jnp.sum(prob[0:layer + 1], axis=0, keepdims=True) - prob[0:1]

    f = lb + (1.0 - lb) * _sigmoid(f_ref[...])
    lf = jnp.log(f)
    kx = 1.0 - f
    q = q_ref[...]
    qs = q * _sigmoid(q)
    v = i_ref[...]
    tri = jnp.where(_chunk_tri(ROWS), 1.0, 0.0).astype(BF16)
    b = _dot_exact_lhs(tri, lf)
    bref = _chunk_rows(b, CHUNK // 2, CHUNK + CHUNK // 2)
    blast = _chunk_rows(b, CHUNK - 1, 2 * CHUNK - 1)
    q_t = qs * jnp.exp(b - bref)
    k_t = kx * jnp.exp(bref - b)
    q_in = qs * jnp.exp(b)
    k_st = kx * jnp.exp(blast - b)
    e_last = jnp.exp(blast)

    rr = _iota((2 * ROWS, ROWS), 0)
    cc = _iota((2 * ROWS, ROWS), 1)
    valid = ((rr // ROWS) == (cc // CHUNK)) & ((cc % CHUNK) <= (rr % CHUNK))
    lane = _iota((CHUNK, LANES), 1)
    m1 = lane < 64
    br = _iota((LANES, LANES), 0)
    bc = _iota((LANES, LANES), 1)
    blockmask = (br // 64) == (bc // 64)

    for p in range(HALF_W // LANES):
        sl = slice(p * LANES, (p + 1) * LANES)
        sc = _dot_nt(_stack_heads(q_t[:, sl]), k_t[:, sl])
        oi = _dot(jnp.where(valid, sc, 0.0), v[:, sl])
        state = s_ref[p]
        for c in range(2):
            rs = slice(c * CHUNK, (c + 1) * CHUNK)
            o_inter = _dot_nt(q_in[rs, sl], state)
            o_intra = (jnp.where(m1, oi[c * 128:c * 128 + 64], 0.0)
                       + jnp.where(m1, 0.0, oi[c * 128 + 64:c * 128 + 128]))
            y_ref[rs, sl] = o_inter + o_intra
            upd = jnp.where(blockmask, _dot_tn(v[rs, sl], k_st[rs, sl]), 0.0)
            state = state * e_last[c * CHUNK:c * CHUNK + 1, sl] + upd
        s_ref[p] = state

    o = y_ref[...]
    ms = _dot_exact_rhs(o * o, _seg_ones(HALF_W, 64)) * (1.0 / 64)
    gate = gt_ref[...]
    y_ref[...] = o * lax.rsqrt(ms + NORM_EPS) * ng_ref[...] * (gate * _sigmoid(gate))


def _hgrn(z_c, lb_logits, norm_g, layer, batch, seq_len):
    n = z_c.shape[0]
    nblk = seq_len // ROWS
    return pl.pallas_call(
        functools.partial(_hgrn_body, layer=layer),
        grid=(batch, nblk),
        in_specs=[_seq_spec(HALF_W, nblk, c) for c in range(4)]
        + [_const_spec(lb_logits.shape), _const_spec((1, HALF_W))],
        out_specs=_seq_spec(HALF_W, nblk),
        out_shape=jax.ShapeDtypeStruct((n, HALF_W), F32),
        scratch_shapes=[pltpu.VMEM((HALF_W // LANES, LANES, LANES), F32)],
        compiler_params=_params(("parallel", "arbitrary")),
        name="hgrn",
    )(z_c, z_c, z_c, z_c, lb_logits, norm_g)


def _gdn_prep_body(z_ref, zh_ref, ba_ref, cw_ref, alog_ref, dtb_ref, eb_ref, ea_ref,
                   q_o, k_o, v_o, beta_o, g_o, *, tiles_per_batch):
    keep = (pl.program_id(0) % tiles_per_batch) != 0
    ext = _with_halo(z_ref[...], zh_ref[...], keep)
    cw = cw_ref[...]
    acc = _shifted(ext, 0) * cw[D_CONV - 1:D_CONV]
    for j in range(D_CONV - 1):
        acc = acc + _shifted(ext, D_CONV - 1 - j) * cw[j:j + 1]
    qkv = acc * _sigmoid(acc)
    seg = _seg_ones(HALF_W, D_HD)
    q = qkv[:, 0:HALF_W]
    k = qkv[:, HALF_W:2 * HALF_W]
    q_o[...] = q * lax.rsqrt(_dot_exact_rhs(q * q, seg) + L2_EPS) * (D_HD ** -0.5)
    k_o[...] = k * lax.rsqrt(_dot_exact_rhs(k * k, seg) + L2_EPS)
    v_o[...] = qkv[:, 2 * HALF_W:3 * HALF_W]
    ba = ba_ref[...]
    beta_o[...] = _sigmoid(_dot_exact_rhs(ba, eb_ref[...]))
    a_full = _dot_exact_rhs(ba, ea_ref[...])
    g_o[...] = -jnp.exp(alog_ref[...]) * _softplus(a_full + dtb_ref[...])


def _gdn_prep(z_qkv, z_ba, conv_w, a_log_full, dt_bias_full, e_b, e_a, seq_len, tm=256):
    n = z_qkv.shape[0]
    w3 = 3 * HALF_W
    return pl.pallas_call(
        functools.partial(_gdn_prep_body, tiles_per_batch=seq_len // tm),
        grid=(n // tm,),
        in_specs=[_row_spec(tm, w3), _halo_spec(tm, w3), _row_spec(tm, LANES),
                  _const_spec((D_CONV, w3)), _const_spec((1, HALF_W)), _const_spec((1, HALF_W)),
                  _const_spec((LANES, HALF_W)), _const_spec((LANES, HALF_W))],
        out_specs=[_row_spec(tm, HALF_W)] * 5,
        out_shape=[jax.ShapeDtypeStruct((n, HALF_W), F32)] * 5,
        compiler_params=_params(("parallel",)),
        name="gdn_prep",
    )(z_qkv, z_qkv, z_ba, conv_w, a_log_full, dt_bias_full, e_b, e_a)


def _gdn_rec_body(q_ref, k_ref, v_ref, beta_ref, g_ref, z_ref, ng_ref, y_ref, s_ref):
    @pl.when(pl.program_id(1) == 0)
    def _():
        s_ref[...] = jnp.zeros_like(s_ref)

    tri = jnp.where(_chunk_tri(ROWS), 1.0, 0.0).astype(BF16)
    incl = _chunk_tri(ROWS)
    strict = _chunk_tri(ROWS, strict=True)
    gc_all = _dot_exact_lhs(tri, g_ref[...])
    glast_all = _chunk_rows(gc_all, CHUNK - 1, 2 * CHUNK - 1)
    for h in range(D_HEADS):
        sl = slice(h * D_HD, (h + 1) * D_HD)
        gc = gc_all[:, sl]
        diff = gc - gc.T
        decay = jnp.where(incl, jnp.exp(jnp.where(incl, diff, 0.0)), 0.0)
        q = q_ref[:, sl]
        k = k_ref[:, sl]
        beta = beta_ref[:, sl]
        kb = k * beta
        kq = _dot_nt(jnp.concatenate([kb, q], axis=0), k)
        lower = jnp.where(strict, kq[0:ROWS] * decay, 0.0)
        qk = kq[ROWS:2 * ROWS] * decay
        t_inv = _neumann_inverse(-lower)
        egc = jnp.exp(gc)
        sol = _dot(t_inv, jnp.concatenate([v_ref[:, sl] * beta, kb * egc], axis=1))
        u, w = sol[:, 0:D_HD], sol[:, D_HD:2 * D_HD]
        q_in = q * egc
        glast = glast_all[:, sl]
        k_st = k * jnp.exp(glast - gc)
        d_last = jnp.exp(glast)
        state = s_ref[h]
        zeros = jnp.zeros((CHUNK, D_HD), F32)
        for c in range(2):
            rs = slice(c * CHUNK, (c + 1) * CHUNK)
            v_new = u[rs] - _dot(w[rs], state)
            v_pad = jnp.concatenate([v_new, zeros] if c == 0 else [zeros, v_new], axis=0)
            o = _dot(q_in[rs], state) + _dot(qk[rs], v_pad)
            state = state * d_last[c * CHUNK:c * CHUNK + 1] + _dot_tn(k_st[rs], v_new)
            z = z_ref[rs, sl]
            y_ref[rs, sl] = _rms(o, ng_ref[...]) * (z * _sigmoid(z))
        s_ref[h] = state


def _gdn_rec(q, k, v, beta, g, z, norm_g, batch, seq_len):
    n = q.shape[0]
    nblk = seq_len // ROWS
    spec = _seq_spec(HALF_W, nblk)
    return pl.pallas_call(
        _gdn_rec_body,
        grid=(batch, nblk),
        in_specs=[spec] * 6 + [_const_spec((1, D_HD))],
        out_specs=spec,
        out_shape=jax.ShapeDtypeStruct((n, HALF_W), F32),
        scratch_shapes=[pltpu.VMEM((D_HEADS, D_HD, D_HD), F32)],
        compiler_params=_params(("parallel", "arbitrary")),
        name="gdn_rec",
    )(q, k, v, beta, g, z, norm_g)


def _outproj_body(x_ref, ya_ref, yb_ref, w_ref, g_ref, o_ref):
    mix = (_dot(ya_ref[...].astype(BF16), w_ref[0:HALF_W, :])
           + _dot(yb_ref[...].astype(BF16), w_ref[HALF_W:2 * HALF_W, :]))
    o_ref[...] = x_ref[...] + _rms(mix, g_ref[...])


def _outproj(x, ya, yb, w, g, tm=512):
    n, d = x.shape
    return pl.pallas_call(
        _outproj_body,
        grid=(n // tm,),
        in_specs=[_row_spec(tm, d), _row_spec(tm, HALF_W), _row_spec(tm, HALF_W),
                  _const_spec((2 * HALF_W, d)), _const_spec((1, d))],
        out_specs=_row_spec(tm, d),
        out_shape=jax.ShapeDtypeStruct((n, d), F32),
        compiler_params=_params(("parallel",)),
        name="outproj",
    )(x, ya, yb, w, g)


def _gelu_tanh(x):
    return 0.5 * x * (1.0 + jnp.tanh((2.0 / jnp.pi) ** 0.5 * (x + 0.044715 * (x * x * x))))


def _ffn_body(x_ref, xh_ref, gpre_ref, win_ref, cw_ref, cb_ref, wout_ref, gpost_ref, o_ref, *, tiles_per_batch):
    keep = (pl.program_id(0) % tiles_per_batch) != 0
    x = x_ref[...]
    g_pre = gpre_ref[...]
    h = _rms(x, g_pre).astype(BF16)
    hh = _rms(xh_ref[...], g_pre).astype(BF16)
    gate = _dot(h, win_ref[:, 0:D_FF])
    gate_halo = _dot(hh, win_ref[:, 0:D_FF])
    up = _dot(h, win_ref[:, D_FF:2 * D_FF])
    ext = _with_halo(gate, gate_halo, keep)
    cw = cw_ref[...]
    conv = gate * cw[2:3] + _shifted(ext, 1) * cw[1:2] + _shifted(ext, 2) * cw[0:1] + cb_ref[...]
    act = (_gelu_tanh(conv) * up).astype(BF16)
    y = _dot(act, wout_ref[...])
    o_ref[...] = x + _rms(y, gpost_ref[...])


def _ffn(x, g_pre, w_in, conv_w, conv_b, w_out, g_post, seq_len, tm=256):
    n, d = x.shape
    return pl.pallas_call(
        functools.partial(_ffn_body, tiles_per_batch=seq_len // tm),
        grid=(n // tm,),
        in_specs=[_row_spec(tm, d), _halo_spec(tm, d), _const_spec((1, d)),
                  _const_spec((d, 2 * D_FF)), _const_spec((3, D_FF)), _const_spec((1, D_FF)),
                  _const_spec((D_FF, d)), _const_spec((1, d))],
        out_specs=_row_spec(tm, d),
        out_shape=jax.ShapeDtypeStruct((n, d), F32),
        compiler_params=_params(("parallel",)),
        name="ffn",
    )(x, x, g_pre, w_in, conv_w, conv_b, w_out, g_post)


def _pad_rows(w, start, total):
    return jnp.zeros((total, w.shape[1]), w.dtype).at[start:start + w.shape[0]].set(w)


def _even_params(e, ev_w_in, rwkv_mu, rwkv_w0, rwkv_w_up, rwkv_a0, rwkv_a_up, rwkv_g_up, rwkv_k_k, rwkv_k_a,
                 rwkv_vres_down, rwkv_vres_up, rwkv_v0):
    a_cols = 3 * HALF_W + A_DECAY_R + A_AAA_R + A_GATE_R
    lr_w = a_cols - 3 * HALF_W
    w = ev_w_in[e]
    d = w.shape[0]
    vres_down = rwkv_vres_down[e - 1] if e > 0 else jnp.zeros((d, A_MV_R), F32)
    lr_block = jnp.concatenate([w[:, 3 * HALF_W:a_cols], vres_down,
                                jnp.zeros((d, A_LR_W - lr_w - A_MV_R), F32)], axis=1)
    w_pad = jnp.concatenate([w[:, 0:3 * HALF_W], lr_block, w[:, a_cols:]], axis=1).astype(BF16)
    mu = rwkv_mu[e]
    o_w, o_a, o_g, o_v = 0, A_DECAY_R, A_DECAY_R + A_AAA_R, lr_w
    prm = {
        "mu": mu[None, 0:3 * HALF_W],
        "mul": jnp.concatenate([mu[3 * HALF_W:], jnp.zeros((A_LR_W - lr_w,), F32)])[None],
        "w0": rwkv_w0[e][None],
        "wup": _pad_rows(rwkv_w_up[e], o_w, A_LR_W),
        "a0": rwkv_a0[e][None],
        "aup": _pad_rows(rwkv_a_up[e], o_a, A_LR_W),
        "gup": _pad_rows(rwkv_g_up[e], o_g, A_LR_W),
        "k_k": rwkv_k_k[e][None],
        "k_a": rwkv_k_a[e][None],
    }
    if e > 0:
        prm["v0"] = rwkv_v0[e - 1][None]
        prm["vup"] = _pad_rows(rwkv_vres_up[e - 1], o_v, A_LR_W)
    return w_pad, prm


def _odd_in_weight(w):
    d = w.shape[0]
    n_ba = 2 * D_HEADS
    return jnp.concatenate([w, jnp.zeros((d, LANES - n_ba), F32)], axis=1).astype(BF16)


def _head_expand(offset):
    r = jnp.arange(LANES)[:, None]
    c = jnp.arange(HALF_W)[None, :]
    return (r == offset + c // D_HD).astype(BF16)


def kernel(x, norm_mix_pre, norm_mix_post, norm_ffn_pre, norm_ffn_post, ev_w_in, ev_w_out, rwkv_mu, rwkv_w0, rwkv_w_up, rwkv_a0, rwkv_a_up, rwkv_g_up, rwkv_k_k, rwkv_k_a, rwkv_r_k, rwkv_ln_g, rwkv_ln_b, rwkv_vres_down, rwkv_vres_up, rwkv_v0, sgu_ln_g, sgu_ln_b, sgu_w, sgu_b, od_w_in, od_w_out, hgrn_lb_logits, hgrn_norm_g, gdn_conv_w, gdn_a_log, gdn_dt_bias, gdn_norm_g, ffn_w_in, ffn_conv_w, ffn_conv_b, ffn_w_out):
    batch, seq_len, d = x.shape
    depth = norm_mix_pre.shape[0]
    n = batch * seq_len
    xf = x.reshape(n, d)
    v_first = None
    for l in range(depth):
        if l % 2 == 0:
            e = l // 2
            w_pad, prm = _even_params(e, ev_w_in, rwkv_mu, rwkv_w0, rwkv_w_up, rwkv_a0, rwkv_a_up, rwkv_g_up,
                                      rwkv_k_k, rwkv_k_a, rwkv_vres_down, rwkv_vres_up, rwkv_v0)
            z_rkv, z_lr, z_b = _inproj(xf, norm_mix_pre[l][None], w_pad, (3 * HALF_W, A_LR_W, 2 * HALF_W))
            r, lw, kmod, v, kk, a, g = _rwkv_prep(z_rkv, z_lr, v_first if e > 0 else None, prm, seq_len)
            if e == 0:
                v_first = v
            y = _rwkv_rec(r, lw, kmod, v, kk, a, batch, seq_len)
            ya = _rwkv_post(y, r, kmod, v, g, rwkv_ln_g[e][None], rwkv_ln_b[e][None],
                            rwkv_r_k[e].reshape(1, HALF_W))
            w_s = sgu_w[e]
            w_pairs = jnp.concatenate([w_s[0::2], w_s[1::2]], axis=2)
            bias = jnp.repeat(sgu_b[e].T, HALF_W // B_GROUPS, axis=1)
            yb = _sgu(z_b, sgu_ln_g[e][None], sgu_ln_b[e][None], w_pairs, bias)
            w_out = ev_w_out[e]
        else:
            o = l // 2
            z_c, z_qkv, z_z, z_ba = _inproj(xf, norm_mix_pre[l][None], _odd_in_weight(od_w_in[o]),
                                            (4 * HALF_W, 3 * HALF_W, HALF_W, LANES))
            ya = _hgrn(z_c, hgrn_lb_logits, hgrn_norm_g[o][None], o, batch, seq_len)
            q, k, v, beta, g = _gdn_prep(z_qkv, z_ba, gdn_conv_w[o],
                                         jnp.repeat(gdn_a_log[o], D_HD)[None],
                                         jnp.repeat(gdn_dt_bias[o], D_HD)[None],
                                         _head_expand(0), _head_expand(D_HEADS), seq_len)
            yb = _gdn_rec(q, k, v, beta, g, z_z, gdn_norm_g[o][None], batch, seq_len)
            w_out = od_w_out[o]
        xf = _outproj(xf, ya, yb, w_out.astype(BF16), norm_mix_post[l][None])
        xf = _ffn(xf, norm_ffn_pre[l][None], ffn_w_in[l].astype(BF16), ffn_conv_w[l], ffn_conv_b[l][None],
                  ffn_w_out[l].astype(BF16), norm_ffn_post[l][None], seq_len)
    return xf.reshape(batch, seq_len, d)
```

```python
import functools

import jax
import jax.numpy as jnp
from jax import lax
from jax.experimental import pallas as pl
from jax.experimental.pallas import tpu as pltpu

F32 = jnp.float32
BF16 = jnp.bfloat16

D_MODEL = 1024
CHUNK = 64
HALF_W = 512
A_DECAY_R = 32
A_AAA_R = 32
A_MV_R = 32
A_GATE_R = 96
A_LR_W = 256
SGU_BLOCK = 128
B_GROUPS = 8
D_HEADS = 4
D_HD = 128
D_CONV = 4
D_FF = 2816
NORM_EPS = 1e-6
RWKV_LN_EPS = 64e-5
SGU_LN_EPS = 1e-5
L2_EPS = 1e-6

ROWS = 128
HALO = 8
LANES = 128
VMEM_LIMIT = 56 * 1024 * 1024


def _dot(a, b):
    return jnp.dot(a.astype(BF16), b.astype(BF16), preferred_element_type=F32)


def _dot_nt(a, b):
    return lax.dot_general(a.astype(BF16), b.astype(BF16), (((1,), (1,)), ((), ())), preferred_element_type=F32)


def _dot_tn(a, b):
    return lax.dot_general(a.astype(BF16), b.astype(BF16), (((0,), (0,)), ((), ())), preferred_element_type=F32)


def _dot_exact_lhs(m_bf16, x):
    hi = x.astype(BF16)
    lo = (x - hi.astype(F32)).astype(BF16)
    return _dot(m_bf16, hi) + _dot(m_bf16, lo)


def _dot_exact_rhs(x, m_bf16):
    hi = x.astype(BF16)
    lo = (x - hi.astype(F32)).astype(BF16)
    return _dot(hi, m_bf16) + _dot(lo, m_bf16)


def _sigmoid(x):
    return 1.0 / (1.0 + jnp.exp(-x))


def _softplus(x):
    return jnp.maximum(x, 0.0) + jnp.log(1.0 + jnp.exp(-jnp.abs(x)))


def _rms(x, g):
    return x * lax.rsqrt(jnp.mean(x * x, axis=-1, keepdims=True) + NORM_EPS) * g


def _iota(shape, dim):
    return lax.broadcasted_iota(jnp.int32, shape, dim)


def _chunk_tri(n, strict=False):
    r = _iota((n, n), 0)
    c = _iota((n, n), 1)
    same = (r // CHUNK) == (c // CHUNK)
    return same & ((c < r) if strict else (c <= r))


def _seg_ones(width, seg):
    r = _iota((width, width), 0)
    c = _iota((width, width), 1)
    return jnp.where((r // seg) == (c // seg), 1.0, 0.0).astype(BF16)


def _stack_heads(x):
    lane = _iota((CHUNK, LANES), 1)
    m1 = lane < 64
    top, bot = x[0:CHUNK], x[CHUNK:2 * CHUNK]
    return jnp.concatenate([jnp.where(m1, top, 0.0), jnp.where(m1, 0.0, top),
                            jnp.where(m1, bot, 0.0), jnp.where(m1, 0.0, bot)], axis=0)


def _dup_heads(x):
    top, bot = x[0:CHUNK], x[CHUNK:2 * CHUNK]
    return jnp.concatenate([top, top, bot, bot], axis=0)


def _neumann_inverse(n_mats, eye, to_bd):
    ps = [eye + n for n in n_mats]
    ms = list(n_mats)
    bds = [to_bd(m) for m in ms]
    for _ in range(5):
        ms = [_dot(m, bd) for m, bd in zip(ms, bds)]
        bds = [to_bd(m) for m in ms]
        ps = [p + _dot(p, bd) for p, bd in zip(ps, bds)]
    return ps


def _chunk_rows(x, r0, r1):
    w = x.shape[1]
    return jnp.concatenate([jnp.broadcast_to(x[r0:r0 + 1], (CHUNK, w)),
                            jnp.broadcast_to(x[r1:r1 + 1], (CHUNK, w))], axis=0)


def _with_halo(x, halo, keep):
    return jnp.concatenate([jnp.where(keep, halo, 0.0), x], axis=0)


def _shifted(ext, s):
    if s == 0:
        return ext[HALO:]
    return pltpu.roll(ext, s, 0)[HALO:]


def _params(sem):
    return pltpu.CompilerParams(dimension_semantics=sem, vmem_limit_bytes=VMEM_LIMIT)


def _row_spec(tm, width, col=0):
    return pl.BlockSpec((tm, width), lambda i, col=col: (i, col))


def _halo_spec(tm, width, col=0):
    step = tm // HALO
    return pl.BlockSpec((HALO, width), lambda i, col=col: (jnp.maximum(i * step - 1, 0), col))


def _const_spec(shape):
    return pl.BlockSpec(shape, lambda *_: (0,) * len(shape))


def _seq_spec(width, nblk, col=0):
    return pl.BlockSpec((ROWS, width), lambda b, j, col=col: (b * nblk + j, col))


def _inproj_body(x_ref, g_ref, w_ref, *o_refs, widths):
    h = _rms(x_ref[...], g_ref[...]).astype(BF16)
    off = 0
    for o_ref, wd in zip(o_refs, widths):
        o_ref[...] = _dot(h, w_ref[:, off:off + wd])
        off += wd


def _inproj(x, g, w, widths, tm=256):
    n, d = x.shape
    wt = w.shape[1]
    return pl.pallas_call(
        functools.partial(_inproj_body, widths=widths),
        grid=(n // tm,),
        in_specs=[_row_spec(tm, d), _const_spec((1, d)), _const_spec((d, wt))],
        out_specs=[_row_spec(tm, wd) for wd in widths],
        out_shape=[jax.ShapeDtypeStruct((n, wd), F32) for wd in widths],
        compiler_params=_params(("parallel",)),
        name="inproj",
    )(x, g, w)


def _rwkv_prep_body(*refs, has_vres, tiles_per_batch):
    if has_vres:
        (z_ref, zh_ref, l_ref, lh_ref, vf_ref, mu_ref, mul_ref, w0_ref, wup_ref, a0_ref, aup_ref, gup_ref,
         kk_ref, ka_ref, v0_ref, vup_ref,
         r_o, lw_o, k_o, v_o, kk_o, a_o, g_o) = refs
    else:
        (z_ref, zh_ref, l_ref, lh_ref, mu_ref, mul_ref, w0_ref, wup_ref, a0_ref, aup_ref, gup_ref,
         kk_ref, ka_ref,
         r_o, lw_o, k_o, v_o, kk_o, a_o, g_o) = refs
    keep = (pl.program_id(0) % tiles_per_batch) != 0
    z = z_ref[...]
    zprev = _shifted(_with_halo(z, zh_ref[...], keep), 1)
    za = z + mu_ref[...] * (zprev - z)
    zl = l_ref[...]
    zlprev = _shifted(_with_halo(zl, lh_ref[...], keep), 1)
    zl = zl + mul_ref[...] * (zlprev - zl)

    r = za[:, 0:HALF_W]
    k = za[:, HALF_W:2 * HALF_W]
    v = za[:, 2 * HALF_W:3 * HALF_W]
    if has_vres:
        gate = _sigmoid(v0_ref[...] + _dot(zl, vup_ref[...]))
        v = v + (vf_ref[...] - v) * gate
    w_log = -_softplus(-(w0_ref[...] + _dot(jnp.tanh(zl), wup_ref[...]))) - 0.5
    lw = -jnp.exp(w_log)
    a = _sigmoid(a0_ref[...] + _dot(zl, aup_ref[...]))
    g = _dot(_sigmoid(zl), gup_ref[...])
    kkk = k * kk_ref[...]
    ss = _dot_exact_rhs(kkk * kkk, _seg_ones(HALF_W, 64))
    kk = kkk * lax.rsqrt(ss + L2_EPS)
    kmod = k * (1.0 + (a - 1.0) * ka_ref[...])
    r_o[...] = r
    lw_o[...] = lw
    k_o[...] = kmod
    v_o[...] = v
    kk_o[...] = kk
    a_o[...] = a
    g_o[...] = g


def _rwkv_prep(z_rkv, z_lr, v_first, prm, seq_len, tm=256):
    n = z_rkv.shape[0]
    has_vres = v_first is not None
    w3 = 3 * HALF_W
    ins = [z_rkv, z_rkv, z_lr, z_lr]
    specs = [_row_spec(tm, w3), _halo_spec(tm, w3), _row_spec(tm, A_LR_W), _halo_spec(tm, A_LR_W)]
    if has_vres:
        ins.append(v_first)
        specs.append(_row_spec(tm, HALF_W))
    names = ["mu", "mul", "w0", "wup", "a0", "aup", "gup", "k_k", "k_a"] + (["v0", "vup"] if has_vres else [])
    for nm in names:
        ins.append(prm[nm])
        specs.append(_const_spec(prm[nm].shape))
    return pl.pallas_call(
        functools.partial(_rwkv_prep_body, has_vres=has_vres, tiles_per_batch=seq_len // tm),
        grid=(n // tm,),
        in_specs=specs,
        out_specs=[_row_spec(tm, HALF_W)] * 7,
        out_shape=[jax.ShapeDtypeStruct((n, HALF_W), F32)] * 7,
        compiler_params=_params(("parallel",)),
        name="rwkv_prep",
    )(*ins)


def _rwkv_rec_body(r_ref, lw_ref, k_ref, v_ref, kk_ref, a_ref, y_ref, s_ref, *, nb):
    @pl.when(pl.program_id(1) == 0)
    def _():
        s_ref[...] = jnp.zeros_like(s_ref)

    pairs = range(HALF_W // LANES)
    sls = [slice(p * LANES, (p + 1) * LANES) for p in pairs]
    tri = jnp.where(_chunk_tri(ROWS), 1.0, 0.0).astype(BF16)
    rr = _iota((ROWS, 2 * ROWS), 0)
    cc = _iota((ROWS, 2 * ROWS), 1)
    same_chunk = (rr // CHUNK) == (cc // ROWS)
    rc_strict = same_chunk & ((cc % CHUNK) < (rr % CHUNK))
    rc_incl = same_chunk & ((cc % CHUNK) <= (rr % CHUNK))
    eye_rc = jnp.where(same_chunk & ((cc % CHUNK) == (rr % CHUNK)), 1.0, 0.0)
    bd_mask = (_iota((2 * ROWS, 2 * ROWS), 0) // CHUNK) == (_iota((2 * ROWS, 2 * ROWS), 1) // CHUNK)
    m1 = _iota((CHUNK, LANES), 1) < 64

    def to_bd(m_rc):
        return jnp.where(bd_mask, _dup_heads(m_rc), 0.0).astype(BF16)

    units = [(blk, p) for blk in range(nb) for p in pairs]
    q, v_s, a_in_s, r_in, bk_t, e_col = {}, {}, {}, {}, {}, {}
    for blk in range(nb):
        rows = slice(blk * ROWS, (blk + 1) * ROWS)
        lw = lw_ref[rows, :]
        b = _dot_exact_lhs(tri, lw)
        bref = _chunk_rows(b, CHUNK // 2, CHUNK + CHUNK // 2)
        blast = _chunk_rows(b, CHUNK - 1, 2 * CHUNK - 1)
        e_pos = jnp.exp(b - bref)
        e_neg = jnp.exp(bref - b)
        e_in = jnp.exp(b)
        e_st = jnp.exp(blast - b)
        e_last = jnp.exp(blast)
        kk = kk_ref[rows, :]
        kka = kk * a_ref[rows, :]
        r = r_ref[rows, :]
        k = k_ref[rows, :]
        neg_kk_exc = -(kk * jnp.exp(-lw))
        a_t = neg_kk_exc * e_pos
        a_in = neg_kk_exc * e_in
        r_t = r * e_pos
        r_in_blk = r * e_in
        b_t = kka * e_neg
        b_st = kka * e_st
        k_t = k * e_neg
        k_st = k * e_st
        v = v_ref[rows, :]
        for p in pairs:
            sl = sls[p]
            un = (blk, p)
            q[un] = _dot_nt(jnp.concatenate([a_t[:, sl], r_t[:, sl]], axis=0),
                            jnp.concatenate([_stack_heads(b_t[:, sl]), _stack_heads(k_t[:, sl])], axis=0))
            v_s[un] = _stack_heads(v[:, sl]).astype(BF16)
            a_in_s[un] = _stack_heads(a_in[:, sl])
            r_in[un] = r_in_blk[:, sl]
            b_s = _stack_heads(b_st[:, sl])
            k_s = _stack_heads(k_st[:, sl])
            for c in range(2):
                ss = slice(c * LANES, (c + 1) * LANES)
                bk_t[un + (c,)] = jnp.concatenate([b_s[ss].T, k_s[ss].T], axis=1).astype(BF16)
                e_col[un + (c,)] = jnp.broadcast_to(e_last[c * CHUNK:c * CHUNK + 1, sl], (LANES, LANES)).T
    a_rb = {u: jnp.where(rc_incl, q[u][ROWS:2 * ROWS, 0:256], 0.0).astype(BF16) for u in units}
    a_kk = {u: jnp.concatenate([jnp.where(rc_strict, q[u][0:ROWS, 256:512], 0.0),
                                jnp.where(rc_incl, q[u][ROWS:2 * ROWS, 256:512], 0.0)], axis=0) for u in units}
    t_list = _neumann_inverse([jnp.where(rc_strict, q[u][0:ROWS, 0:256], 0.0) for u in units], eye_rc, to_bd)
    t_inv = dict(zip(units, t_list))
    av = {u: _dot(a_kk[u], v_s[u]) for u in units}
    tt = {u: _dot(t_inv[u], jnp.concatenate([a_in_s[u], _stack_heads(av[u][0:ROWS])], axis=1))
          for u in units}
    state = [s_ref[p] for p in pairs]
    for blk in range(nb):
        for c in range(2):
            rs = slice(c * CHUNK, (c + 1) * CHUNK)
            ss = slice(c * LANES, (c + 1) * LANES)
            out_rows = slice(blk * ROWS + c * CHUNK, blk * ROWS + (c + 1) * CHUNK)
            for p in pairs:
                un = (blk, p)
                x = _dot(jnp.concatenate([tt[un][rs, 0:LANES], r_in[un][rs]], axis=0), state[p])
                u = tt[un][rs, LANES:2 * LANES] + x[0:CHUNK]
                u_s = jnp.concatenate([jnp.where(m1, u, 0.0), jnp.where(m1, 0.0, u)], axis=0).astype(BF16)
                y_ref[out_rows, sls[p]] = (x[CHUNK:2 * CHUNK] + _dot(a_rb[un][rs, ss], u_s)
                                           + av[un][ROWS + c * CHUNK:ROWS + (c + 1) * CHUNK])
                upd = _dot(bk_t[un + (c,)], jnp.concatenate([u_s, v_s[un][ss]], axis=0))
                state[p] = state[p] * e_col[un + (c,)] + upd
    for p in pairs:
        s_ref[p] = state[p]


def _rwkv_rec(r, lw, k, v, kk, a, batch, seq_len, nb=2):
    n = r.shape[0]
    rows = nb * ROWS
    nblk = seq_len // rows
    spec = pl.BlockSpec((rows, HALF_W), lambda b, j: (b * nblk + j, 0))
    return pl.pallas_call(
        functools.partial(_rwkv_rec_body, nb=nb),
        grid=(batch, nblk),
        in_specs=[spec] * 6,
        out_specs=spec,
        out_shape=jax.ShapeDtypeStruct((n, HALF_W), F32),
        scratch_shapes=[pltpu.VMEM((HALF_W // LANES, LANES, LANES), F32)],
        compiler_params=_params(("parallel", "arbitrary")),
        name="rwkv_rec",
    )(r, lw, k, v, kk, a)


def _rwkv_post_body(y_ref, r_ref, k_ref, v_ref, g_ref, lng_ref, lnb_ref, rk_ref, o_ref):
    seg = _seg_ones(HALF_W, 64)
    y = y_ref[...]
    mu = _dot_exact_rhs(y, seg) * (1.0 / 64)
    d = y - mu
    var = _dot_exact_rhs(d * d, seg) * (1.0 / 64)
    yn = d * lax.rsqrt(var + RWKV_LN_EPS) * lng_ref[...] + lnb_ref[...]
    bonus = _dot_exact_rhs(r_ref[...] * k_ref[...] * rk_ref[...], seg)
    o_ref[...] = (yn + bonus * v_ref[...]) * g_ref[...]


def _rwkv_post(y, r, k, v, g, ln_g, ln_b, r_k, tm=512):
    n = y.shape[0]
    return pl.pallas_call(
        _rwkv_post_body,
        grid=(n // tm,),
        in_specs=[_row_spec(tm, HALF_W)] * 5 + [_const_spec((1, HALF_W))] * 3,
        out_specs=_row_spec(tm, HALF_W),
        out_shape=jax.ShapeDtypeStruct((n, HALF_W), F32),
        compiler_params=_params(("parallel",)),
        name="rwkv_post",
    )(y, r, k, v, g, ln_g, ln_b, r_k)


def _gelu_erf(x):
    return 0.5 * x * (1.0 + lax.erf(x * (2.0 ** -0.5)))


def _sgu_body(u_ref, v_ref, lng_ref, lnb_ref, w_ref, bias_ref, o_ref, *, tm):
    seg = _seg_ones(HALF_W, 64)
    u = _gelu_erf(u_ref[...])
    v = _gelu_erf(v_ref[...])
    mu = _dot_exact_rhs(v, seg) * (1.0 / 64)
    d = v - mu
    var = _dot_exact_rhs(d * d, seg) * (1.0 / 64)
    vn = d * lax.rsqrt(var + SGU_LN_EPS) * lng_ref[...] + lnb_ref[...]
    r = _iota((SGU_BLOCK, 2 * SGU_BLOCK), 0)
    c = _iota((SGU_BLOCK, 2 * SGU_BLOCK), 1)
    causal = (r // CHUNK) >= ((c % SGU_BLOCK) // CHUNK)
    lane = _iota((SGU_BLOCK, LANES), 1)
    m1 = lane < 64
    bias = bias_ref[...]
    for p in range(HALF_W // LANES):
        sl = slice(p * LANES, (p + 1) * LANES)
        w = jnp.where(causal, w_ref[p], 0.0)
        for nb in range(tm // SGU_BLOCK):
            rs = slice(nb * SGU_BLOCK, (nb + 1) * SGU_BLOCK)
            blk = vn[rs, sl]
            stacked = jnp.concatenate([jnp.where(m1, blk, 0.0), jnp.where(m1, 0.0, blk)], axis=0)
            mixed = _dot(w, stacked) + bias[:, sl]
            o_ref[rs, sl] = u[rs, sl] * mixed


def _sgu(z_b, ln_g, ln_b, w_pairs, bias, tm=256):
    n = z_b.shape[0]
    return pl.pallas_call(
        functools.partial(_sgu_body, tm=tm),
        grid=(n // tm,),
        in_specs=[_row_spec(tm, HALF_W, 0), _row_spec(tm, HALF_W, 1),
                  _const_spec((1, HALF_W)), _const_spec((1, HALF_W)),
                  _const_spec(w_pairs.shape), _const_spec(bias.shape)],
        out_specs=_row_spec(tm, HALF_W),
        out_shape=jax.ShapeDtypeStruct((n, HALF_W), F32),
        compiler_params=_params(("parallel",)),
        name="sgu",
    )(z_b, z_b, ln_g, ln_b, w_pairs, bias)


def _hgrn_body(q_ref, f_ref, i_ref, gt_ref, lbl_ref, ng_ref, y_ref, s_ref, *, layer):
    @pl.when(pl.program_id(1) == 0)
    def _():
        s_ref[...] = jnp.zeros_like(s_ref)

    logits = lbl_ref[...]
    e = jnp.exp(logits - jnp.max(logits, axis=0, keepdims=True))
    prob = e / jnp.sum(e, axis=0, keepdims=True)
    lb = jnp.sum(prob[0:layer + 1], axis=0, keepdims=True) - prob[0:1]

    f = lb + (1.0 - lb) * _sigmoid(f_ref[...])
    lf = jnp.log(f)
    kx = 1.0 - f
    q = q_ref[...]
    qs = q * _sigmoid(q)
    v = i_ref[...]
    tri = jnp.where(_chunk_tri(ROWS), 1.0, 0.0).astype(BF16)
    b = _dot_exact_lhs(tri, lf)
    bref = _chunk_rows(b, CHUNK // 2, CHUNK + CHUNK // 2)
    blast = _chunk_rows(b, CHUNK - 1, 2 * CHUNK - 1)
    q_t = qs * jnp.exp(b - bref)
    k_t = kx * jnp.exp(bref - b)
    q_in = qs * jnp.exp(b)
    k_st = kx * jnp.exp(blast - b)
    e_last = jnp.exp(blast)

    rr = _iota((2 * ROWS, ROWS), 0)
    cc = _iota((2 * ROWS, ROWS), 1)
    valid = ((rr // ROWS) == (cc // CHUNK)) & ((cc % CHUNK) <= (rr % CHUNK))
    lane = _iota((CHUNK, LANES), 1)
    m1 = lane < 64
    br = _iota((LANES, LANES), 0)
    bc = _iota((LANES, LANES), 1)
    blockmask = (br // 64) == (bc // 64)

    for p in range(HALF_W // LANES):
        sl = slice(p * LANES, (p + 1) * LANES)
        sc = _dot_nt(_stack_heads(q_t[:, sl]), k_t[:, sl])
        oi = _dot(jnp.where(valid, sc, 0.0), v[:, sl])
        state = s_ref[p]
        for c in range(2):
            rs = slice(c * CHUNK, (c + 1) * CHUNK)
            o_inter = _dot_nt(q_in[rs, sl], state)
            o_intra = (jnp.where(m1, oi[c * 128:c * 128 + 64], 0.0)
                       + jnp.where(m1, 0.0, oi[c * 128 + 64:c * 128 + 128]))
            y_ref[rs, sl] = o_inter + o_intra
            upd = jnp.where(blockmask, _dot_tn(v[rs, sl], k_st[rs, sl]), 0.0)
            state = state * e_last[c * CHUNK:c * CHUNK + 1, sl] + upd
        s_ref[p] = state

    o = y_ref[...]
    ms = _dot_exact_rhs(o * o, _seg_ones(HALF_W, 64)) * (1.0 / 64)
    gate = gt_ref[...]
    y_ref[...] = o * lax.rsqrt(ms + NORM_EPS) * ng_ref[...] * (gate * _sigmoid(gate))


def _hgrn(z_c, lb_logits, norm_g, layer, batch, seq_len):
    n = z_c.shape[0]
    nblk = seq_len // ROWS
    return pl.pallas_call(
        functools.partial(_hgrn_body, layer=layer),
        grid=(batch, nblk),
        in_specs=[_seq_spec(HALF_W, nblk, c) for c in range(4)]
        + [_const_spec(lb_logits.shape), _const_spec((1, HALF_W))],
        out_specs=_seq_spec(HALF_W, nblk),
        out_shape=jax.ShapeDtypeStruct((n, HALF_W), F32),
        scratch_shapes=[pltpu.VMEM((HALF_W // LANES, LANES, LANES), F32)],
        compiler_params=_params(("parallel", "arbitrary")),
        name="hgrn",
    )(z_c, z_c, z_c, z_c, lb_logits, norm_g)


def _gdn_prep_body(z_ref, zh_ref, ba_ref, cw_ref, alog_ref, dtb_ref, eb_ref, ea_ref,
                   q_o, k_o, v_o, beta_o, g_o, *, tiles_per_batch):
    keep = (pl.program_id(0) % tiles_per_batch) != 0
    ext = _with_halo(z_ref[...], zh_ref[...], keep)
    cw = cw_ref[...]
    acc = _shifted(ext, 0) * cw[D_CONV - 1:D_CONV]
    for j in range(D_CONV - 1):
        acc = acc + _shifted(ext, D_CONV - 1 - j) * cw[j:j + 1]
    qkv = acc * _sigmoid(acc)
    seg = _seg_ones(HALF_W, D_HD)
    q = qkv[:, 0:HALF_W]
    k = qkv[:, HALF_W:2 * HALF_W]
    q_o[...] = q * lax.rsqrt(_dot_exact_rhs(q * q, seg) + L2_EPS) * (D_HD ** -0.5)
    k_o[...] = k * lax.rsqrt(_dot_exact_rhs(k * k, seg) + L2_EPS)
    v_o[...] = qkv[:, 2 * HALF_W:3 * HALF_W]
    ba = ba_ref[...]
    beta_o[...] = _sigmoid(_dot_exact_rhs(ba, eb_ref[...]))
    a_full = _dot_exact_rhs(ba, ea_ref[...])
    g_o[...] = -jnp.exp(alog_ref[...]) * _softplus(a_full + dtb_ref[...])


def _gdn_prep(z_qkv, z_ba, conv_w, a_log_full, dt_bias_full, e_b, e_a, seq_len, tm=256):
    n = z_qkv.shape[0]
    w3 = 3 * HALF_W
    return pl.pallas_call(
        functools.partial(_gdn_prep_body, tiles_per_batch=seq_len // tm),
        grid=(n // tm,),
        in_specs=[_row_spec(tm, w3), _halo_spec(tm, w3), _row_spec(tm, LANES),
                  _const_spec((D_CONV, w3)), _const_spec((1, HALF_W)), _const_spec((1, HALF_W)),
                  _const_spec((LANES, HALF_W)), _const_spec((LANES, HALF_W))],
        out_specs=[_row_spec(tm, HALF_W)] * 5,
        out_shape=[jax.ShapeDtypeStruct((n, HALF_W), F32)] * 5,
        compiler_params=_params(("parallel",)),
        name="gdn_prep",
    )(z_qkv, z_qkv, z_ba, conv_w, a_log_full, dt_bias_full, e_b, e_a)


def _gdn_rec_body(q_ref, k_ref, v_ref, beta_ref, g_ref, z_ref, ng_ref, y_ref, s_ref, *, nb):
    @pl.when(pl.program_id(1) == 0)
    def _():
        s_ref[...] = jnp.zeros_like(s_ref)

    tri = jnp.where(_chunk_tri(ROWS), 1.0, 0.0).astype(BF16)
    incl = _chunk_tri(ROWS)
    strict = _chunk_tri(ROWS, strict=True)
    heads = range(D_HEADS)
    sls = [slice(h * D_HD, (h + 1) * D_HD) for h in heads]
    eye = jnp.where(_iota((ROWS, ROWS), 0) == _iota((ROWS, ROWS), 1), 1.0, 0.0)
    zero_blk = jnp.zeros((ROWS, D_HD), BF16)
    zeros = jnp.zeros((CHUNK, D_HD), F32)

    def to_bd(m_rc):
        mb = m_rc.astype(BF16)
        return jnp.concatenate([jnp.concatenate([mb[:, 0:D_HD], zero_blk], axis=1),
                                jnp.concatenate([zero_blk, mb[:, D_HD:2 * D_HD]], axis=1)], axis=0)

    units = [(blk, h) for blk in range(nb) for h in heads]
    neg_lower, qk, rhs, wq, k_st_t, d_last = {}, {}, {}, {}, {}, {}
    for blk in range(nb):
        rows = slice(blk * ROWS, (blk + 1) * ROWS)
        gc_all = _dot_exact_lhs(tri, g_ref[rows, :])
        glast_all = _chunk_rows(gc_all, CHUNK - 1, 2 * CHUNK - 1)
        for h in heads:
            un = (blk, h)
            gc = gc_all[:, sls[h]]
            glast = glast_all[:, sls[h]]
            diff = gc - gc.T
            decay = jnp.where(incl, jnp.exp(jnp.where(incl, diff, 0.0)), 0.0)
            q = q_ref[rows, sls[h]]
            k = k_ref[rows, sls[h]]
            beta = beta_ref[rows, sls[h]]
            kb = k * beta
            kq = _dot_nt(jnp.concatenate([kb, q], axis=0), k)
            neg_lower[un] = -jnp.where(strict, kq[0:ROWS] * decay, 0.0)
            qk[un] = (kq[ROWS:2 * ROWS] * decay).astype(BF16)
            egc = jnp.exp(gc)
            rhs[un] = jnp.concatenate([v_ref[rows, sls[h]] * beta, kb * egc], axis=1)
            wq[un] = q * egc
            k_st_t[un] = (k * jnp.exp(glast - gc)).T.astype(BF16)
            d_last[un] = jnp.exp(glast)
    pair_units = [(blk, h) for blk in range(nb) for h in range(0, D_HEADS, 2)]
    t_pairs = _neumann_inverse([jnp.concatenate([neg_lower[(blk, h)], neg_lower[(blk, h + 1)]], axis=1)
                                for blk, h in pair_units], jnp.concatenate([eye, eye], axis=1), to_bd)
    t_pairs = dict(zip(pair_units, t_pairs))
    sol = {(blk, h): _dot(t_pairs[(blk, h - h % 2)][:, (h % 2) * D_HD:(h % 2 + 1) * D_HD], rhs[(blk, h)])
           for blk, h in units}
    state = [s_ref[h] for h in heads]
    for blk in range(nb):
        for c in range(2):
            rs = slice(c * CHUNK, (c + 1) * CHUNK)
            out_rows = slice(blk * ROWS + c * CHUNK, blk * ROWS + (c + 1) * CHUNK)
            for h in heads:
                un = (blk, h)
                x = _dot(jnp.concatenate([sol[un][rs, D_HD:2 * D_HD], wq[un][rs]], axis=0), state[h])
                v_new = sol[un][rs, 0:D_HD] - x[0:CHUNK]
                v_pad = jnp.concatenate([v_new, zeros] if c == 0 else [zeros, v_new], axis=0).astype(BF16)
                o = x[CHUNK:2 * CHUNK] + _dot(qk[un][rs], v_pad)
                state[h] = state[h] * d_last[un][c * CHUNK:c * CHUNK + 1] + _dot(k_st_t[un], v_pad)
                z = z_ref[out_rows, sls[h]]
                y_ref[out_rows, sls[h]] = _rms(o, ng_ref[...]) * (z * _sigmoid(z))
    for h in heads:
        s_ref[h] = state[h]


def _gdn_rec(q, k, v, beta, g, z, norm_g, batch, seq_len, nb=4):
    n = q.shape[0]
    rows = nb * ROWS
    nblk = seq_len // rows
    spec = pl.BlockSpec((rows, HALF_W), lambda b, j: (b * nblk + j, 0))
    return pl.pallas_call(
        functools.partial(_gdn_rec_body, nb=nb),
        grid=(batch, nblk),
        in_specs=[spec] * 6 + [_const_spec((1, D_HD))],
        out_specs=spec,
        out_shape=jax.ShapeDtypeStruct((n, HALF_W), F32),
        scratch_shapes=[pltpu.VMEM((D_HEADS, D_HD, D_HD), F32)],
        compiler_params=_params(("parallel", "arbitrary")),
        name="gdn_rec",
    )(q, k, v, beta, g, z, norm_g)


def _outproj_body(x_ref, ya_ref, yb_ref, w_ref, g_ref, o_ref):
    mix = (_dot(ya_ref[...].astype(BF16), w_ref[0:HALF_W, :])
           + _dot(yb_ref[...].astype(BF16), w_ref[HALF_W:2 * HALF_W, :]))
    o_ref[...] = x_ref[...] + _rms(mix, g_ref[...])


def _outproj(x, ya, yb, w, g, tm=512):
    n, d = x.shape
    return pl.pallas_call(
        _outproj_body,
        grid=(n // tm,),
        in_specs=[_row_spec(tm, d), _row_spec(tm, HALF_W), _row_spec(tm, HALF_W),
                  _const_spec((2 * HALF_W, d)), _const_spec((1, d))],
        out_specs=_row_spec(tm, d),
        out_shape=jax.ShapeDtypeStruct((n, d), F32),
        compiler_params=_params(("parallel",)),
        name="outproj",
    )(x, ya, yb, w, g)


def _gelu_tanh(x):
    return 0.5 * x * (1.0 + jnp.tanh((2.0 / jnp.pi) ** 0.5 * (x + 0.044715 * (x * x * x))))


def _ffn_body(x_ref, xh_ref, gpre_ref, win_ref, cw_ref, cb_ref, wout_ref, gpost_ref, o_ref, *, tiles_per_batch):
    keep = (pl.program_id(0) % tiles_per_batch) != 0
    x = x_ref[...]
    g_pre = gpre_ref[...]
    h = _rms(x, g_pre).astype(BF16)
    hh = _rms(xh_ref[...], g_pre).astype(BF16)
    gate = _dot(h, win_ref[:, 0:D_FF])
    gate_halo = _dot(hh, win_ref[:, 0:D_FF])
    up = _dot(h, win_ref[:, D_FF:2 * D_FF])
    ext = _with_halo(gate, gate_halo, keep)
    cw = cw_ref[...]
    conv = gate * cw[2:3] + _shifted(ext, 1) * cw[1:2] + _shifted(ext, 2) * cw[0:1] + cb_ref[...]
    act = (_gelu_tanh(conv) * up).astype(BF16)
    y = _dot(act, wout_ref[...])
    o_ref[...] = x + _rms(y, gpost_ref[...])


def _ffn(x, g_pre, w_in, conv_w, conv_b, w_out, g_post, seq_len, tm=256):
    n, d = x.shape
    return pl.pallas_call(
        functools.partial(_ffn_body, tiles_per_batch=seq_len // tm),
        grid=(n // tm,),
        in_specs=[_row_spec(tm, d), _halo_spec(tm, d), _const_spec((1, d)),
                  _const_spec((d, 2 * D_FF)), _const_spec((3, D_FF)), _const_spec((1, D_FF)),
                  _const_spec((D_FF, d)), _const_spec((1, d))],
        out_specs=_row_spec(tm, d),
        out_shape=jax.ShapeDtypeStruct((n, d), F32),
        compiler_params=_params(("parallel",)),
        name="ffn",
    )(x, x, g_pre, w_in, conv_w, conv_b, w_out, g_post)


def _pad_rows(w, start, total):
    return jnp.zeros((total, w.shape[1]), w.dtype).at[start:start + w.shape[0]].set(w)


def _even_params(e, ev_w_in, rwkv_mu, rwkv_w0, rwkv_w_up, rwkv_a0, rwkv_a_up, rwkv_g_up, rwkv_k_k, rwkv_k_a,
                 rwkv_vres_down, rwkv_vres_up, rwkv_v0):
    a_cols = 3 * HALF_W + A_DECAY_R + A_AAA_R + A_GATE_R
    lr_w = a_cols - 3 * HALF_W
    w = ev_w_in[e]
    d = w.shape[0]
    vres_down = rwkv_vres_down[e - 1] if e > 0 else jnp.zeros((d, A_MV_R), F32)
    lr_block = jnp.concatenate([w[:, 3 * HALF_W:a_cols], vres_down,
                                jnp.zeros((d, A_LR_W - lr_w - A_MV_R), F32)], axis=1)
    w_pad = jnp.concatenate([w[:, 0:3 * HALF_W], lr_block, w[:, a_cols:]], axis=1).astype(BF16)
    mu = rwkv_mu[e]
    o_w, o_a, o_g, o_v = 0, A_DECAY_R, A_DECAY_R + A_AAA_R, lr_w
    prm = {
        "mu": mu[None, 0:3 * HALF_W],
        "mul": jnp.concatenate([mu[3 * HALF_W:], jnp.zeros((A_LR_W - lr_w,), F32)])[None],
        "w0": rwkv_w0[e][None],
        "wup": _pad_rows(rwkv_w_up[e], o_w, A_LR_W),
        "a0": rwkv_a0[e][None],
        "aup": _pad_rows(rwkv_a_up[e], o_a, A_LR_W),
        "gup": _pad_rows(rwkv_g_up[e], o_g, A_LR_W),
        "k_k": rwkv_k_k[e][None],
        "k_a": rwkv_k_a[e][None],
    }
    if e > 0:
        prm["v0"] = rwkv_v0[e - 1][None]
        prm["vup"] = _pad_rows(rwkv_vres_up[e - 1], o_v, A_LR_W)
    return w_pad, prm


def _odd_in_weight(w):
    d = w.shape[0]
    n_ba = 2 * D_HEADS
    return jnp.concatenate([w, jnp.zeros((d, LANES - n_ba), F32)], axis=1).astype(BF16)


def _head_expand(offset):
    r = jnp.arange(LANES)[:, None]
    c = jnp.arange(HALF_W)[None, :]
    return (r == offset + c // D_HD).astype(BF16)


def kernel(x, norm_mix_pre, norm_mix_post, norm_ffn_pre, norm_ffn_post, ev_w_in, ev_w_out, rwkv_mu, rwkv_w0, rwkv_w_up, rwkv_a0, rwkv_a_up, rwkv_g_up, rwkv_k_k, rwkv_k_a, rwkv_r_k, rwkv_ln_g, rwkv_ln_b, rwkv_vres_down, rwkv_vres_up, rwkv_v0, sgu_ln_g, sgu_ln_b, sgu_w, sgu_b, od_w_in, od_w_out, hgrn_lb_logits, hgrn_norm_g, gdn_conv_w, gdn_a_log, gdn_dt_bias, gdn_norm_g, ffn_w_in, ffn_conv_w, ffn_conv_b, ffn_w_out):
    batch, seq_len, d = x.shape
    depth = norm_mix_pre.shape[0]
    n = batch * seq_len
    xf = x.reshape(n, d)
    v_first = None
    for l in range(depth):
        if l % 2 == 0:
            e = l // 2
            w_pad, prm = _even_params(e, ev_w_in, rwkv_mu, rwkv_w0, rwkv_w_up, rwkv_a0, rwkv_a_up, rwkv_g_up,
                                      rwkv_k_k, rwkv_k_a, rwkv_vres_down, rwkv_vres_up, rwkv_v0)
            z_rkv, z_lr, z_b = _inproj(xf, norm_mix_pre[l][None], w_pad, (3 * HALF_W, A_LR_W, 2 * HALF_W))
            r, lw, kmod, v, kk, a, g = _rwkv_prep(z_rkv, z_lr, v_first if e > 0 else None, prm, seq_len)
            if e == 0:
                v_first = v
            y = _rwkv_rec(r, lw, kmod, v, kk, a, batch, seq_len)
            ya = _rwkv_post(y, r, kmod, v, g, rwkv_ln_g[e][None], rwkv_ln_b[e][None],
                            rwkv_r_k[e].reshape(1, HALF_W))
            w_s = sgu_w[e]
            w_pairs = jnp.concatenate([w_s[0::2], w_s[1::2]], axis=2)
            bias = jnp.repeat(sgu_b[e].T, HALF_W // B_GROUPS, axis=1)
            yb = _sgu(z_b, sgu_ln_g[e][None], sgu_ln_b[e][None], w_pairs, bias)
            w_out = ev_w_out[e]
        else:
            o = l // 2
            z_c, z_qkv, z_z, z_ba = _inproj(xf, norm_mix_pre[l][None], _odd_in_weight(od_w_in[o]),
                                            (4 * HALF_W, 3 * HALF_W, HALF_W, LANES))
            ya = _hgrn(z_c, hgrn_lb_logits, hgrn_norm_g[o][None], o, batch, seq_len)
            q, k, v, beta, g = _gdn_prep(z_qkv, z_ba, gdn_conv_w[o],
                                         jnp.repeat(gdn_a_log[o], D_HD)[None],
                                         jnp.repeat(gdn_dt_bias[o], D_HD)[None],
                                         _head_expand(0), _head_expand(D_HEADS), seq_len)
            yb = _gdn_rec(q, k, v, beta, g, z_z, gdn_norm_g[o][None], batch, seq_len)
            w_out = od_w_out[o]
        xf = _outproj(xf, ya, yb, w_out.astype(BF16), norm_mix_post[l][None])
        xf = _ffn(xf, norm_ffn_pre[l][None], ffn_w_in[l].astype(BF16), ffn_conv_w[l], ffn_conv_b[l][None],
                  ffn_w_out[l].astype(BF16), norm_ffn_post[l][None], seq_len)
    return xf.reshape(batch, seq_len, d)
```

```python
import functools

import jax
import jax.numpy as jnp
from jax import lax
from jax.experimental import pallas as pl
from jax.experimental.pallas import tpu as pltpu

F32 = jnp.float32
BF16 = jnp.bfloat16

D_MODEL = 1024
CHUNK = 64
HALF_W = 512
A_DECAY_R = 32
A_AAA_R = 32
A_MV_R = 32
A_GATE_R = 96
A_LR_W = 256
SGU_BLOCK = 128
B_GROUPS = 8
D_HEADS = 4
D_HD = 128
D_CONV = 4
D_FF = 2816
NORM_EPS = 1e-6
RWKV_LN_EPS = 64e-5
SGU_LN_EPS = 1e-5
L2_EPS = 1e-6

ROWS = 128
HALO = 16
FF_SPLIT = 2
LANES = 128
VMEM_LIMIT = 56 * 1024 * 1024


def _dot(a, b):
    return jnp.dot(a.astype(BF16), b.astype(BF16), preferred_element_type=F32)


def _dot_nt(a, b):
    return lax.dot_general(a.astype(BF16), b.astype(BF16), (((1,), (1,)), ((), ())), preferred_element_type=F32)


def _dot_tn(a, b):
    return lax.dot_general(a.astype(BF16), b.astype(BF16), (((0,), (0,)), ((), ())), preferred_element_type=F32)


def _dot_exact_lhs(m_bf16, x):
    hi = x.astype(BF16)
    lo = (x - hi.astype(F32)).astype(BF16)
    return _dot(m_bf16, hi) + _dot(m_bf16, lo)


def _dot_exact_rhs(x, m_bf16):
    hi = x.astype(BF16)
    lo = (x - hi.astype(F32)).astype(BF16)
    return _dot(hi, m_bf16) + _dot(lo, m_bf16)


def _sigmoid(x):
    return 1.0 / (1.0 + jnp.exp(-x))


def _softplus(x):
    return jnp.maximum(x, 0.0) + jnp.log(1.0 + jnp.exp(-jnp.abs(x)))


def _rms(x, g):
    return x * lax.rsqrt(jnp.mean(x * x, axis=-1, keepdims=True) + NORM_EPS) * g


def _iota(shape, dim):
    return lax.broadcasted_iota(jnp.int32, shape, dim)


def _chunk_tri(n, strict=False):
    r = _iota((n, n), 0)
    c = _iota((n, n), 1)
    same = (r // CHUNK) == (c // CHUNK)
    return same & ((c < r) if strict else (c <= r))


def _seg_ones(width, seg):
    r = _iota((width, width), 0)
    c = _iota((width, width), 1)
    return jnp.where((r // seg) == (c // seg), 1.0, 0.0).astype(BF16)


def _stack_heads(x):
    lane = _iota((CHUNK, LANES), 1)
    m1 = lane < 64
    top, bot = x[0:CHUNK], x[CHUNK:2 * CHUNK]
    return jnp.concatenate([jnp.where(m1, top, 0.0), jnp.where(m1, 0.0, top),
                            jnp.where(m1, bot, 0.0), jnp.where(m1, 0.0, bot)], axis=0)


def _dup_heads(x):
    top, bot = x[0:CHUNK], x[CHUNK:2 * CHUNK]
    return jnp.concatenate([top, top, bot, bot], axis=0)


def _neumann_inverse(n_mats, eye, to_bd):
    ps = [eye + n for n in n_mats]
    ms = list(n_mats)
    bds = [to_bd(m) for m in ms]
    for _ in range(5):
        ms = [_dot(m, bd) for m, bd in zip(ms, bds)]
        bds = [to_bd(m) for m in ms]
        ps = [p + _dot(p, bd) for p, bd in zip(ps, bds)]
    return ps


def _chunk_rows(x, r0, r1):
    w = x.shape[1]
    return jnp.concatenate([jnp.broadcast_to(x[r0:r0 + 1], (CHUNK, w)),
                            jnp.broadcast_to(x[r1:r1 + 1], (CHUNK, w))], axis=0)


def _with_halo(x, halo, keep):
    return jnp.concatenate([jnp.where(keep, halo, 0.0), x], axis=0)


def _shifted(ext, s):
    if s == 0:
        return ext[HALO:]
    return pltpu.roll(ext, s, 0)[HALO:]


def _params(sem):
    return pltpu.CompilerParams(dimension_semantics=sem, vmem_limit_bytes=VMEM_LIMIT)


def _layer_spec(shape, layer):
    return pl.BlockSpec((None,) + tuple(shape[1:]), lambda *_: (layer, 0, 0), pipeline_mode=pl.Buffered(1))


def _row_spec(tm, width, col=0):
    return pl.BlockSpec((tm, width), lambda i, col=col: (i, col))


def _halo_spec(tm, width, col=0):
    step = tm // HALO
    return pl.BlockSpec((HALO, width), lambda i, col=col: (jnp.maximum(i * step - 1, 0), col))


def _const_spec(shape):
    return pl.BlockSpec(shape, lambda *_: (0,) * len(shape), pipeline_mode=pl.Buffered(1))


def _inproj_body(x_ref, g_ref, w_ref, *o_refs, widths):
    h = _rms(x_ref[...], g_ref[...]).astype(BF16)
    off = 0
    for o_ref, wd in zip(o_refs, widths):
        o_ref[...] = _dot(h, w_ref[:, off:off + wd])
        off += wd


def _inproj(x, g, w_all, layer, widths, tm=512):
    n, d = x.shape
    return pl.pallas_call(
        functools.partial(_inproj_body, widths=widths),
        grid=(n // tm,),
        in_specs=[_row_spec(tm, d), _const_spec((1, d)), _layer_spec(w_all.shape, layer)],
        out_specs=[_row_spec(tm, wd) for wd in widths],
        out_shape=[jax.ShapeDtypeStruct((n, wd), F32) for wd in widths],
        compiler_params=_params(("parallel",)),
        name="inproj",
    )(x, g, w_all)


def _rwkv_prep_body(*refs, has_vres, tiles_per_batch):
    if has_vres:
        (z_ref, zh_ref, l_ref, lh_ref, vf_ref, mu_ref, mul_ref, w0_ref, wup_ref, a0_ref, aup_ref, gup_ref,
         kk_ref, ka_ref, v0_ref, vup_ref,
         r_o, lw_o, k_o, v_o, kk_o, a_o, g_o) = refs
    else:
        (z_ref, zh_ref, l_ref, lh_ref, mu_ref, mul_ref, w0_ref, wup_ref, a0_ref, aup_ref, gup_ref,
         kk_ref, ka_ref,
         r_o, lw_o, k_o, v_o, kk_o, a_o, g_o) = refs
    keep = (pl.program_id(0) % tiles_per_batch) != 0
    z = z_ref[...]
    zprev = _shifted(_with_halo(z, zh_ref[...], keep), 1)
    za = z + mu_ref[...] * (zprev - z)
    zl = l_ref[...]
    zlprev = _shifted(_with_halo(zl, lh_ref[...], keep), 1)
    zl = zl + mul_ref[...] * (zlprev - zl)

    r = za[:, 0:HALF_W]
    k = za[:, HALF_W:2 * HALF_W]
    v = za[:, 2 * HALF_W:3 * HALF_W]
    if has_vres:
        gate = _sigmoid(v0_ref[...] + _dot(zl, vup_ref[...]))
        v = v + (vf_ref[...] - v) * gate
    w_log = -_softplus(-(w0_ref[...] + _dot(jnp.tanh(zl), wup_ref[...]))) - 0.5
    lw = -jnp.exp(w_log)
    a = _sigmoid(a0_ref[...] + _dot(zl, aup_ref[...]))
    g = _dot(_sigmoid(zl), gup_ref[...])
    kkk = k * kk_ref[...]
    ss = _dot(kkk * kkk, _seg_ones(HALF_W, 64))
    kk = kkk * lax.rsqrt(ss + L2_EPS)
    kmod = k * (1.0 + (a - 1.0) * ka_ref[...])
    r_o[...] = r
    lw_o[...] = lw
    k_o[...] = kmod
    v_o[...] = v
    kk_o[...] = kk
    a_o[...] = a
    g_o[...] = g


def _rwkv_prep(z_rkv, z_lr, v_first, prm, seq_len, tm=512):
    n = z_rkv.shape[0]
    has_vres = v_first is not None
    w3 = 3 * HALF_W
    ins = [z_rkv, z_rkv, z_lr, z_lr]
    specs = [_row_spec(tm, w3), _halo_spec(tm, w3), _row_spec(tm, A_LR_W), _halo_spec(tm, A_LR_W)]
    if has_vres:
        ins.append(v_first)
        specs.append(_row_spec(tm, HALF_W))
    names = ["mu", "mul", "w0", "wup", "a0", "aup", "gup", "k_k", "k_a"] + (["v0", "vup"] if has_vres else [])
    for nm in names:
        ins.append(prm[nm])
        specs.append(_const_spec(prm[nm].shape))
    return pl.pallas_call(
        functools.partial(_rwkv_prep_body, has_vres=has_vres, tiles_per_batch=seq_len // tm),
        grid=(n // tm,),
        in_specs=specs,
        out_specs=[_row_spec(tm, HALF_W)] * 7,
        out_shape=[jax.ShapeDtypeStruct((n, HALF_W), F32)] * 7,
        compiler_params=_params(("parallel",)),
        name="rwkv_prep",
    )(*ins)


def _rwkv_rec_body(r_ref, lw_ref, k_ref, v_ref, kk_ref, a_ref, y_ref, s_ref, *, nb):
    @pl.when(pl.program_id(1) == 0)
    def _():
        s_ref[...] = jnp.zeros_like(s_ref)

    pairs = range(HALF_W // LANES)
    sls = [slice(p * LANES, (p + 1) * LANES) for p in pairs]
    tri = jnp.where(_chunk_tri(ROWS), 1.0, 0.0).astype(BF16)
    rr = _iota((ROWS, 2 * ROWS), 0)
    cc = _iota((ROWS, 2 * ROWS), 1)
    same_chunk = (rr // CHUNK) == (cc // ROWS)
    rc_strict = same_chunk & ((cc % CHUNK) < (rr % CHUNK))
    rc_incl = same_chunk & ((cc % CHUNK) <= (rr % CHUNK))
    eye_rc = jnp.where(same_chunk & ((cc % CHUNK) == (rr % CHUNK)), 1.0, 0.0)
    bd_mask = (_iota((2 * ROWS, 2 * ROWS), 0) // CHUNK) == (_iota((2 * ROWS, 2 * ROWS), 1) // CHUNK)
    m1 = _iota((CHUNK, LANES), 1) < 64

    def to_bd(m_rc):
        return jnp.where(bd_mask, _dup_heads(m_rc), 0.0).astype(BF16)

    units = [(blk, p) for blk in range(nb) for p in pairs]
    q, v_s, a_in_s, r_in, bk_t, e_col = {}, {}, {}, {}, {}, {}
    for blk in range(nb):
        rows = slice(blk * ROWS, (blk + 1) * ROWS)
        lw = lw_ref[rows, :]
        b = _dot_exact_lhs(tri, lw)
        bref = _chunk_rows(b, CHUNK // 2, CHUNK + CHUNK // 2)
        blast = _chunk_rows(b, CHUNK - 1, 2 * CHUNK - 1)
        e_pos = jnp.exp(b - bref)
        e_neg = jnp.exp(bref - b)
        e_in = jnp.exp(b)
        e_st = jnp.exp(blast - b)
        e_last = jnp.exp(blast)
        kk = kk_ref[rows, :]
        kka = kk * a_ref[rows, :]
        r = r_ref[rows, :]
        k = k_ref[rows, :]
        neg_kk_exc = -(kk * jnp.exp(-lw))
        a_t = neg_kk_exc * e_pos
        a_in = neg_kk_exc * e_in
        r_t = r * e_pos
        r_in_blk = r * e_in
        b_t = kka * e_neg
        b_st = kka * e_st
        k_t = k * e_neg
        k_st = k * e_st
        v = v_ref[rows, :]
        for p in pairs:
            sl = sls[p]
            un = (blk, p)
            q[un] = _dot_nt(jnp.concatenate([a_t[:, sl], r_t[:, sl]], axis=0),
                            jnp.concatenate([_stack_heads(b_t[:, sl]), _stack_heads(k_t[:, sl])], axis=0))
            v_s[un] = _stack_heads(v[:, sl]).astype(BF16)
            a_in_s[un] = _stack_heads(a_in[:, sl])
            r_in[un] = r_in_blk[:, sl]
            b_s = _stack_heads(b_st[:, sl])
            k_s = _stack_heads(k_st[:, sl])
            for c in range(2):
                ss = slice(c * LANES, (c + 1) * LANES)
                bk_t[un + (c,)] = jnp.concatenate([b_s[ss].T, k_s[ss].T], axis=1).astype(BF16)
                e_col[un + (c,)] = jnp.broadcast_to(e_last[c * CHUNK:c * CHUNK + 1, sl], (LANES, LANES)).T
    a_rb = {u: jnp.where(rc_incl, q[u][ROWS:2 * ROWS, 0:256], 0.0).astype(BF16) for u in units}
    a_kk = {u: jnp.concatenate([jnp.where(rc_strict, q[u][0:ROWS, 256:512], 0.0),
                                jnp.where(rc_incl, q[u][ROWS:2 * ROWS, 256:512], 0.0)], axis=0) for u in units}
    t_list = _neumann_inverse([jnp.where(rc_strict, q[u][0:ROWS, 0:256], 0.0) for u in units], eye_rc, to_bd)
    t_inv = dict(zip(units, t_list))
    av = {u: _dot(a_kk[u], v_s[u]) for u in units}
    tt = {u: _dot(t_inv[u], jnp.concatenate([a_in_s[u], _stack_heads(av[u][0:ROWS])], axis=1))
          for u in units}
    state = [s_ref[p] for p in pairs]
    for blk in range(nb):
        for c in range(2):
            rs = slice(c * CHUNK, (c + 1) * CHUNK)
            ss = slice(c * LANES, (c + 1) * LANES)
            out_rows = slice(blk * ROWS + c * CHUNK, blk * ROWS + (c + 1) * CHUNK)
            for p in pairs:
                un = (blk, p)
                x = _dot(jnp.concatenate([tt[un][rs, 0:LANES], r_in[un][rs]], axis=0), state[p])
                u = tt[un][rs, LANES:2 * LANES] + x[0:CHUNK]
                u_s = jnp.concatenate([jnp.where(m1, u, 0.0), jnp.where(m1, 0.0, u)], axis=0).astype(BF16)
                y_ref[out_rows, sls[p]] = (x[CHUNK:2 * CHUNK] + _dot(a_rb[un][rs, ss], u_s)
                                           + av[un][ROWS + c * CHUNK:ROWS + (c + 1) * CHUNK])
                upd = _dot(bk_t[un + (c,)], jnp.concatenate([u_s, v_s[un][ss]], axis=0))
                state[p] = state[p] * e_col[un + (c,)] + upd
    for p in pairs:
        s_ref[p] = state[p]


def _rwkv_rec(r, lw, k, v, kk, a, batch, seq_len, nb=4):
    n = r.shape[0]
    rows = nb * ROWS
    nblk = seq_len // rows
    spec = pl.BlockSpec((rows, HALF_W), lambda b, j: (b * nblk + j, 0))
    return pl.pallas_call(
        functools.partial(_rwkv_rec_body, nb=nb),
        grid=(batch, nblk),
        in_specs=[spec] * 6,
        out_specs=spec,
        out_shape=jax.ShapeDtypeStruct((n, HALF_W), F32),
        scratch_shapes=[pltpu.VMEM((HALF_W // LANES, LANES, LANES), F32)],
        compiler_params=_params(("parallel", "arbitrary")),
        name="rwkv_rec",
    )(r, lw, k, v, kk, a)


def _rwkv_post_body(y_ref, r_ref, k_ref, v_ref, g_ref, lng_ref, lnb_ref, rk_ref, o_ref):
    seg = _seg_ones(HALF_W, 64)
    y = y_ref[...]
    mu = _dot_exact_rhs(y, seg) * (1.0 / 64)
    d = y - mu
    var = _dot(d * d, seg) * (1.0 / 64)
    yn = d * lax.rsqrt(var + RWKV_LN_EPS) * lng_ref[...] + lnb_ref[...]
    bonus = _dot_exact_rhs(r_ref[...] * k_ref[...] * rk_ref[...], seg)
    o_ref[...] = (yn + bonus * v_ref[...]) * g_ref[...]


def _rwkv_post(y, r, k, v, g, ln_g, ln_b, r_k, tm=1024):
    n = y.shape[0]
    return pl.pallas_call(
        _rwkv_post_body,
        grid=(n // tm,),
        in_specs=[_row_spec(tm, HALF_W)] * 5 + [_const_spec((1, HALF_W))] * 3,
        out_specs=_row_spec(tm, HALF_W),
        out_shape=jax.ShapeDtypeStruct((n, HALF_W), F32),
        compiler_params=_params(("parallel",)),
        name="rwkv_post",
    )(y, r, k, v, g, ln_g, ln_b, r_k)


def _gelu_erf(x):
    return 0.5 * x * (1.0 + lax.erf(x * (2.0 ** -0.5)))


def _sgu_body(u_ref, v_ref, lng_ref, lnb_ref, w_ref, bias_ref, o_ref, *, tm):
    seg = _seg_ones(HALF_W, 64)
    u = _gelu_erf(u_ref[...])
    v = _gelu_erf(v_ref[...])
    mu = _dot_exact_rhs(v, seg) * (1.0 / 64)
    d = v - mu
    var = _dot(d * d, seg) * (1.0 / 64)
    vn = d * lax.rsqrt(var + SGU_LN_EPS) * lng_ref[...] + lnb_ref[...]
    r = _iota((SGU_BLOCK, 2 * SGU_BLOCK), 0)
    c = _iota((SGU_BLOCK, 2 * SGU_BLOCK), 1)
    causal = (r // CHUNK) >= ((c % SGU_BLOCK) // CHUNK)
    lane = _iota((SGU_BLOCK, LANES), 1)
    m1 = lane < 64
    bias = bias_ref[...]
    for p in range(HALF_W // LANES):
        sl = slice(p * LANES, (p + 1) * LANES)
        w = jnp.where(causal, w_ref[p], 0.0)
        for nb in range(tm // SGU_BLOCK):
            rs = slice(nb * SGU_BLOCK, (nb + 1) * SGU_BLOCK)
            blk = vn[rs, sl]
            stacked = jnp.concatenate([jnp.where(m1, blk, 0.0), jnp.where(m1, 0.0, blk)], axis=0)
            mixed = _dot(w, stacked) + bias[:, sl]
            o_ref[rs, sl] = u[rs, sl] * mixed


def _sgu(z_b, ln_g, ln_b, w_pairs, bias, tm=512):
    n = z_b.shape[0]
    return pl.pallas_call(
        functools.partial(_sgu_body, tm=tm),
        grid=(n // tm,),
        in_specs=[_row_spec(tm, HALF_W, 0), _row_spec(tm, HALF_W, 1),
                  _const_spec((1, HALF_W)), _const_spec((1, HALF_W)),
                  _const_spec(w_pairs.shape), _const_spec(bias.shape)],
        out_specs=_row_spec(tm, HALF_W),
        out_shape=jax.ShapeDtypeStruct((n, HALF_W), F32),
        compiler_params=_params(("parallel",)),
        name="sgu",
    )(z_b, z_b, ln_g, ln_b, w_pairs, bias)


def _hgrn_body(q_ref, f_ref, i_ref, gt_ref, lbl_ref, ng_ref, y_ref, s_ref, *, layer, nb):
    @pl.when(pl.program_id(1) == 0)
    def _():
        s_ref[...] = jnp.zeros_like(s_ref)

    logits = lbl_ref[...]
    e = jnp.exp(logits - jnp.max(logits, axis=0, keepdims=True))
    prob = e / jnp.sum(e, axis=0, keepdims=True)
    lb = jnp.sum(prob[0:layer + 1], axis=0, keepdims=True) - prob[0:1]

    f = lb + (1.0 - lb) * _sigmoid(f_ref[...])
    lf_all = jnp.log(f)
    kx_all = 1.0 - f
    q = q_ref[...]
    qs_all = q * _sigmoid(q)
    tri = jnp.where(_chunk_tri(ROWS), 1.0, 0.0).astype(BF16)
    rr = _iota((2 * ROWS, ROWS), 0)
    cc = _iota((2 * ROWS, ROWS), 1)
    valid = ((rr // ROWS) == (cc // CHUNK)) & ((cc % CHUNK) <= (rr % CHUNK))
    m1 = _iota((CHUNK, LANES), 1) < 64
    blockmask = (_iota((LANES, LANES), 0) // 64) == (_iota((LANES, LANES), 1) // 64)
    pairs = range(HALF_W // LANES)
    state = [s_ref[p] for p in pairs]

    for blk in range(nb):
        rows = slice(blk * ROWS, (blk + 1) * ROWS)
        qs = qs_all[rows]
        kx = kx_all[rows]
        v = i_ref[rows, :]
        b = _dot_exact_lhs(tri, lf_all[rows])
        bref = _chunk_rows(b, CHUNK // 2, CHUNK + CHUNK // 2)
        blast = _chunk_rows(b, CHUNK - 1, 2 * CHUNK - 1)
        q_t = qs * jnp.exp(b - bref)
        k_t = kx * jnp.exp(bref - b)
        q_in = qs * jnp.exp(b)
        k_st = kx * jnp.exp(blast - b)
        e_last = jnp.exp(blast)
        for p in pairs:
            sl = slice(p * LANES, (p + 1) * LANES)
            sc = _dot_nt(_stack_heads(q_t[:, sl]), k_t[:, sl])
            oi = _dot(jnp.where(valid, sc, 0.0), v[:, sl])
            for c in range(2):
                rs = slice(c * CHUNK, (c + 1) * CHUNK)
                o_inter = _dot_nt(q_in[rs, sl], state[p])
                o_intra = (jnp.where(m1, oi[c * 128:c * 128 + 64], 0.0)
                           + jnp.where(m1, 0.0, oi[c * 128 + 64:c * 128 + 128]))
                y_ref[blk * ROWS + c * CHUNK:blk * ROWS + (c + 1) * CHUNK, sl] = o_inter + o_intra
                upd = jnp.where(blockmask, _dot_tn(v[rs, sl], k_st[rs, sl]), 0.0)
                state[p] = state[p] * e_last[c * CHUNK:c * CHUNK + 1, sl] + upd
    for p in pairs:
        s_ref[p] = state[p]

    o = y_ref[...]
    ms = _dot(o * o, _seg_ones(HALF_W, 64)) * (1.0 / 64)
    gate = gt_ref[...]
    y_ref[...] = o * lax.rsqrt(ms + NORM_EPS) * ng_ref[...] * (gate * _sigmoid(gate))


def _hgrn(z_c, lb_logits, norm_g, layer, batch, seq_len, nb=4):
    n = z_c.shape[0]
    rows = nb * ROWS
    nblk = seq_len // rows

    def spec(col):
        return pl.BlockSpec((rows, HALF_W), lambda b, j: (b * nblk + j, col))

    return pl.pallas_call(
        functools.partial(_hgrn_body, layer=layer, nb=nb),
        grid=(batch, nblk),
        in_specs=[spec(c) for c in range(4)] + [_const_spec(lb_logits.shape), _const_spec((1, HALF_W))],
        out_specs=spec(0),
        out_shape=jax.ShapeDtypeStruct((n, HALF_W), F32),
        scratch_shapes=[pltpu.VMEM((HALF_W // LANES, LANES, LANES), F32)],
        compiler_params=_params(("parallel", "arbitrary")),
        name="hgrn",
    )(z_c, z_c, z_c, z_c, lb_logits, norm_g)


def _gdn_prep_body(z_ref, zh_ref, ba_ref, cw_ref, alog_ref, dtb_ref, eb_ref, ea_ref,
                   q_o, k_o, v_o, beta_o, g_o, *, tiles_per_batch):
    keep = (pl.program_id(0) % tiles_per_batch) != 0
    ext = _with_halo(z_ref[...], zh_ref[...], keep)
    cw = cw_ref[...]
    acc = _shifted(ext, 0) * cw[D_CONV - 1:D_CONV]
    for j in range(D_CONV - 1):
        acc = acc + _shifted(ext, D_CONV - 1 - j) * cw[j:j + 1]
    qkv = acc * _sigmoid(acc)
    seg = _seg_ones(HALF_W, D_HD)
    q = qkv[:, 0:HALF_W]
    k = qkv[:, HALF_W:2 * HALF_W]
    q_o[...] = q * lax.rsqrt(_dot(q * q, seg) + L2_EPS) * (D_HD ** -0.5)
    k_o[...] = k * lax.rsqrt(_dot(k * k, seg) + L2_EPS)
    v_o[...] = qkv[:, 2 * HALF_W:3 * HALF_W]
    ba = ba_ref[...]
    beta_o[...] = _sigmoid(_dot_exact_rhs(ba, eb_ref[...]))
    a_full = _dot_exact_rhs(ba, ea_ref[...])
    g_o[...] = -jnp.exp(alog_ref[...]) * _softplus(a_full + dtb_ref[...])


def _gdn_prep(z_qkv, z_ba, conv_w, a_log_full, dt_bias_full, e_b, e_a, seq_len, tm=512):
    n = z_qkv.shape[0]
    w3 = 3 * HALF_W
    return pl.pallas_call(
        functools.partial(_gdn_prep_body, tiles_per_batch=seq_len // tm),
        grid=(n // tm,),
        in_specs=[_row_spec(tm, w3), _halo_spec(tm, w3), _row_spec(tm, LANES),
                  _const_spec((D_CONV, w3)), _const_spec((1, HALF_W)), _const_spec((1, HALF_W)),
                  _const_spec((LANES, HALF_W)), _const_spec((LANES, HALF_W))],
        out_specs=[_row_spec(tm, HALF_W)] * 5,
        out_shape=[jax.ShapeDtypeStruct((n, HALF_W), F32)] * 5,
        compiler_params=_params(("parallel",)),
        name="gdn_prep",
    )(z_qkv, z_qkv, z_ba, conv_w, a_log_full, dt_bias_full, e_b, e_a)


def _gdn_rec_body(q_ref, k_ref, v_ref, beta_ref, g_ref, z_ref, ng_ref, y_ref, s_ref, *, nb):
    @pl.when(pl.program_id(1) == 0)
    def _():
        s_ref[...] = jnp.zeros_like(s_ref)

    tri = jnp.where(_chunk_tri(ROWS), 1.0, 0.0).astype(BF16)
    incl = _chunk_tri(ROWS)
    strict = _chunk_tri(ROWS, strict=True)
    heads = range(D_HEADS)
    sls = [slice(h * D_HD, (h + 1) * D_HD) for h in heads]
    eye = jnp.where(_iota((ROWS, ROWS), 0) == _iota((ROWS, ROWS), 1), 1.0, 0.0)
    zero_blk = jnp.zeros((ROWS, D_HD), BF16)
    zeros = jnp.zeros((CHUNK, D_HD), F32)

    def to_bd(m_rc):
        mb = m_rc.astype(BF16)
        return jnp.concatenate([jnp.concatenate([mb[:, 0:D_HD], zero_blk], axis=1),
                                jnp.concatenate([zero_blk, mb[:, D_HD:2 * D_HD]], axis=1)], axis=0)

    units = [(blk, h) for blk in range(nb) for h in heads]
    neg_lower, qk, rhs, wq, k_st_t, d_last = {}, {}, {}, {}, {}, {}
    for blk in range(nb):
        rows = slice(blk * ROWS, (blk + 1) * ROWS)
        gc_all = _dot_exact_lhs(tri, g_ref[rows, :])
        glast_all = _chunk_rows(gc_all, CHUNK - 1, 2 * CHUNK - 1)
        for h in heads:
            un = (blk, h)
            gc = gc_all[:, sls[h]]
            glast = glast_all[:, sls[h]]
            diff = gc - gc.T
            decay = jnp.where(incl, jnp.exp(jnp.where(incl, diff, 0.0)), 0.0)
            q = q_ref[rows, sls[h]]
            k = k_ref[rows, sls[h]]
            beta = beta_ref[rows, sls[h]]
            kb = k * beta
            kq = _dot_nt(jnp.concatenate([kb, q], axis=0), k)
            neg_lower[un] = -jnp.where(strict, kq[0:ROWS] * decay, 0.0)
            qk[un] = (kq[ROWS:2 * ROWS] * decay).astype(BF16)
            egc = jnp.exp(gc)
            rhs[un] = jnp.concatenate([v_ref[rows, sls[h]] * beta, kb * egc], axis=1)
            wq[un] = q * egc
            k_st_t[un] = (k * jnp.exp(glast - gc)).T.astype(BF16)
            d_last[un] = jnp.exp(glast)
    pair_units = [(blk, h) for blk in range(nb) for h in range(0, D_HEADS, 2)]
    t_pairs = _neumann_inverse([jnp.concatenate([neg_lower[(blk, h)], neg_lower[(blk, h + 1)]], axis=1)
                                for blk, h in pair_units], jnp.concatenate([eye, eye], axis=1), to_bd)
    t_pairs = dict(zip(pair_units, t_pairs))
    sol = {(blk, h): _dot(t_pairs[(blk, h - h % 2)][:, (h % 2) * D_HD:(h % 2 + 1) * D_HD], rhs[(blk, h)])
           for blk, h in units}
    state = [s_ref[h] for h in heads]
    for blk in range(nb):
        for c in range(2):
            rs = slice(c * CHUNK, (c + 1) * CHUNK)
            out_rows = slice(blk * ROWS + c * CHUNK, blk * ROWS + (c + 1) * CHUNK)
            for h in heads:
                un = (blk, h)
                x = _dot(jnp.concatenate([sol[un][rs, D_HD:2 * D_HD], wq[un][rs]], axis=0), state[h])
                v_new = sol[un][rs, 0:D_HD] - x[0:CHUNK]
                v_pad = jnp.concatenate([v_new, zeros] if c == 0 else [zeros, v_new], axis=0).astype(BF16)
                o = x[CHUNK:2 * CHUNK] + _dot(qk[un][rs], v_pad)
                state[h] = state[h] * d_last[un][c * CHUNK:c * CHUNK + 1] + _dot(k_st_t[un], v_pad)
                z = z_ref[out_rows, sls[h]]
                y_ref[out_rows, sls[h]] = _rms(o, ng_ref[...]) * (z * _sigmoid(z))
    for h in heads:
        s_ref[h] = state[h]


def _gdn_rec(q, k, v, beta, g, z, norm_g, batch, seq_len, nb=4):
    n = q.shape[0]
    rows = nb * ROWS
    nblk = seq_len // rows
    spec = pl.BlockSpec((rows, HALF_W), lambda b, j: (b * nblk + j, 0))
    return pl.pallas_call(
        functools.partial(_gdn_rec_body, nb=nb),
        grid=(batch, nblk),
        in_specs=[spec] * 6 + [_const_spec((1, D_HD))],
        out_specs=spec,
        out_shape=jax.ShapeDtypeStruct((n, HALF_W), F32),
        scratch_shapes=[pltpu.VMEM((D_HEADS, D_HD, D_HD), F32)],
        compiler_params=_params(("parallel", "arbitrary")),
        name="gdn_rec",
    )(q, k, v, beta, g, z, norm_g)


def _gelu_tanh(x):
    return 0.5 * x * (1.0 + jnp.tanh((2.0 / jnp.pi) ** 0.5 * (x + 0.044715 * (x * x * x))))


def _mix_ffn_body(x_ref, xh_ref, ya_ref, yah_ref, yb_ref, ybh_ref, wo_ref, gmix_ref, gpre_ref,
                  win_ref, cw_ref, cb_ref, wout_ref, gpost_ref, o_ref, *, tiles_per_batch):
    keep = (pl.program_id(0) % tiles_per_batch) != 0
    xe = jnp.concatenate([xh_ref[...], x_ref[...]], axis=0)
    yae = jnp.concatenate([yah_ref[...], ya_ref[...]], axis=0)
    ybe = jnp.concatenate([ybh_ref[...], yb_ref[...]], axis=0)
    mix = _dot(yae, wo_ref[0:HALF_W, :]) + _dot(ybe, wo_ref[HALF_W:2 * HALF_W, :])
    x1 = xe + _rms(mix, gmix_ref[...])
    h = _rms(x1, gpre_ref[...]).astype(BF16)
    cw = cw_ref[...]
    cb = cb_ref[...]
    ffw = D_FF // FF_SPLIT
    y = None
    for j in range(FF_SPLIT):
        cols = slice(j * ffw, (j + 1) * ffw)
        gate = _dot(h, win_ref[:, cols])
        ext = jnp.concatenate([jnp.where(keep, gate[0:HALO], 0.0), gate[HALO:]], axis=0)
        up = _dot(h[HALO:], win_ref[:, D_FF + j * ffw:D_FF + (j + 1) * ffw])
        conv = (ext[HALO:] * cw[2:3, cols] + _shifted(ext, 1) * cw[1:2, cols]
                + _shifted(ext, 2) * cw[0:1, cols] + cb[:, cols])
        part = _dot(_gelu_tanh(conv) * up, wout_ref[cols, :])
        y = part if y is None else y + part
    o_ref[...] = x1[HALO:] + _rms(y, gpost_ref[...])


def _mix_ffn(x, ya, yb, w_o_all, mix_layer, g_mix, g_pre, w_in_all, conv_w, conv_b, w_out_all, g_post, layer,
             seq_len, tm=512):
    n, d = x.shape
    return pl.pallas_call(
        functools.partial(_mix_ffn_body, tiles_per_batch=seq_len // tm),
        grid=(n // tm,),
        in_specs=[_row_spec(tm, d), _halo_spec(tm, d),
                  _row_spec(tm, HALF_W), _halo_spec(tm, HALF_W), _row_spec(tm, HALF_W), _halo_spec(tm, HALF_W),
                  _layer_spec(w_o_all.shape, mix_layer), _const_spec((1, d)), _const_spec((1, d)),
                  _layer_spec(w_in_all.shape, layer), _const_spec((3, D_FF)), _const_spec((1, D_FF)),
                  _layer_spec(w_out_all.shape, layer), _const_spec((1, d))],
        out_specs=_row_spec(tm, d),
        out_shape=jax.ShapeDtypeStruct((n, d), F32),
        compiler_params=_params(("parallel",)),
        name="mix_ffn",
    )(x, x, ya, ya, yb, yb, w_o_all, g_mix, g_pre, w_in_all, conv_w, conv_b, w_out_all, g_post)


def _pad_rows(w, start, total):
    return jnp.zeros((total, w.shape[1]), w.dtype).at[start:start + w.shape[0]].set(w)


def _even_in_weights(ev_w_in, rwkv_vres_down):
    a_cols = 3 * HALF_W + A_DECAY_R + A_AAA_R + A_GATE_R
    n_even, d, _ = ev_w_in.shape
    w = ev_w_in.astype(BF16)
    vres = jnp.concatenate([jnp.zeros((1, d, A_MV_R), BF16), rwkv_vres_down.astype(BF16)], axis=0)
    pad = jnp.zeros((n_even, d, A_LR_W - (a_cols - 3 * HALF_W) - A_MV_R), BF16)
    return jnp.concatenate([w[:, :, 0:a_cols], vres, pad, w[:, :, a_cols:]], axis=2)


def _even_params(e, rwkv_mu, rwkv_w0, rwkv_w_up, rwkv_a0, rwkv_a_up, rwkv_g_up, rwkv_k_k, rwkv_k_a,
                 rwkv_vres_up, rwkv_v0):
    lr_w = A_DECAY_R + A_AAA_R + A_GATE_R
    mu = rwkv_mu[e]
    o_w, o_a, o_g, o_v = 0, A_DECAY_R, A_DECAY_R + A_AAA_R, lr_w
    prm = {
        "mu": mu[None, 0:3 * HALF_W],
        "mul": jnp.concatenate([mu[3 * HALF_W:], jnp.zeros((A_LR_W - lr_w,), F32)])[None],
        "w0": rwkv_w0[e][None],
        "wup": _pad_rows(rwkv_w_up[e], o_w, A_LR_W),
        "a0": rwkv_a0[e][None],
        "aup": _pad_rows(rwkv_a_up[e], o_a, A_LR_W),
        "gup": _pad_rows(rwkv_g_up[e], o_g, A_LR_W),
        "k_k": rwkv_k_k[e][None],
        "k_a": rwkv_k_a[e][None],
    }
    if e > 0:
        prm["v0"] = rwkv_v0[e - 1][None]
        prm["vup"] = _pad_rows(rwkv_vres_up[e - 1], o_v, A_LR_W)
    return prm


def _odd_in_weights(od_w_in):
    return jnp.pad(od_w_in.astype(BF16), ((0, 0), (0, 0), (0, LANES - 2 * D_HEADS)))


def _head_expand(offset):
    r = jnp.arange(LANES)[:, None]
    c = jnp.arange(HALF_W)[None, :]
    return (r == offset + c // D_HD).astype(BF16)


def kernel(x, norm_mix_pre, norm_mix_post, norm_ffn_pre, norm_ffn_post, ev_w_in, ev_w_out, rwkv_mu, rwkv_w0, rwkv_w_up, rwkv_a0, rwkv_a_up, rwkv_g_up, rwkv_k_k, rwkv_k_a, rwkv_r_k, rwkv_ln_g, rwkv_ln_b, rwkv_vres_down, rwkv_vres_up, rwkv_v0, sgu_ln_g, sgu_ln_b, sgu_w, sgu_b, od_w_in, od_w_out, hgrn_lb_logits, hgrn_norm_g, gdn_conv_w, gdn_a_log, gdn_dt_bias, gdn_norm_g, ffn_w_in, ffn_conv_w, ffn_conv_b, ffn_w_out):
    batch, seq_len, d = x.shape
    depth = norm_mix_pre.shape[0]
    n = batch * seq_len
    xf = x.reshape(n, d)
    ev_w_in_p = _even_in_weights(ev_w_in, rwkv_vres_down)
    od_w_in_p = _odd_in_weights(od_w_in)
    ev_w_out_b = ev_w_out.astype(BF16)
    od_w_out_b = od_w_out.astype(BF16)
    ffn_w_in_b = ffn_w_in.astype(BF16)
    ffn_w_out_b = ffn_w_out.astype(BF16)
    v_first = None
    for l in range(depth):
        if l % 2 == 0:
            e = l // 2
            prm = _even_params(e, rwkv_mu, rwkv_w0, rwkv_w_up, rwkv_a0, rwkv_a_up, rwkv_g_up, rwkv_k_k, rwkv_k_a,
                               rwkv_vres_up, rwkv_v0)
            z_rkv, z_lr, z_b = _inproj(xf, norm_mix_pre[l][None], ev_w_in_p, e, (3 * HALF_W, A_LR_W, 2 * HALF_W))
            r, lw, kmod, v, kk, a, g = _rwkv_prep(z_rkv, z_lr, v_first if e > 0 else None, prm, seq_len)
            if e == 0:
                v_first = v
            y = _rwkv_rec(r, lw, kmod, v, kk, a, batch, seq_len)
            ya = _rwkv_post(y, r, kmod, v, g, rwkv_ln_g[e][None], rwkv_ln_b[e][None],
                            rwkv_r_k[e].reshape(1, HALF_W))
            w_s = sgu_w[e]
            w_pairs = jnp.concatenate([w_s[0::2], w_s[1::2]], axis=2)
            bias = jnp.repeat(sgu_b[e].T, HALF_W // B_GROUPS, axis=1)
            yb = _sgu(z_b, sgu_ln_g[e][None], sgu_ln_b[e][None], w_pairs, bias)
            w_out_all, mix_layer = ev_w_out_b, e
        else:
            o = l // 2
            z_c, z_qkv, z_z, z_ba = _inproj(xf, norm_mix_pre[l][None], od_w_in_p, o,
                                            (4 * HALF_W, 3 * HALF_W, HALF_W, LANES))
            ya = _hgrn(z_c, hgrn_lb_logits, hgrn_norm_g[o][None], o, batch, seq_len)
            q, k, v, beta, g = _gdn_prep(z_qkv, z_ba, gdn_conv_w[o],
                                         jnp.repeat(gdn_a_log[o], D_HD)[None],
                                         jnp.repeat(gdn_dt_bias[o], D_HD)[None],
                                         _head_expand(0), _head_expand(D_HEADS), seq_len)
            yb = _gdn_rec(q, k, v, beta, g, z_z, gdn_norm_g[o][None], batch, seq_len)
            w_out_all, mix_layer = od_w_out_b, o
        xf = _mix_ffn(xf, ya, yb, w_out_all, mix_layer, norm_mix_post[l][None], norm_ffn_pre[l][None],
                      ffn_w_in_b, ffn_conv_w[l], ffn_conv_b[l][None], ffn_w_out_b, norm_ffn_post[l][None], l,
                      seq_len)
    return xf.reshape(batch, seq_len, d)
```

```python
import functools

import jax
import jax.numpy as jnp
from jax import lax
from jax.experimental import pallas as pl
from jax.experimental.pallas import tpu as pltpu

F32 = jnp.float32
BF16 = jnp.bfloat16

D_MODEL = 1024
CHUNK = 64
HALF_W = 512
A_DECAY_R = 32
A_AAA_R = 32
A_MV_R = 32
A_GATE_R = 96
A_LR_W = 256
SGU_BLOCK = 128
B_GROUPS = 8
D_HEADS = 4
D_HD = 128
D_CONV = 4
D_FF = 2816
NORM_EPS = 1e-6
RWKV_LN_EPS = 64e-5
SGU_LN_EPS = 1e-5
L2_EPS = 1e-6

ROWS = 128
HALO = 16
FF_SPLIT = 2
LANES = 128
VMEM_LIMIT = 56 * 1024 * 1024


def _dot(a, b):
    return jnp.dot(a.astype(BF16), b.astype(BF16), preferred_element_type=F32)


def _dot_nt(a, b):
    return lax.dot_general(a.astype(BF16), b.astype(BF16), (((1,), (1,)), ((), ())), preferred_element_type=F32)


def _dot_tn(a, b):
    return lax.dot_general(a.astype(BF16), b.astype(BF16), (((0,), (0,)), ((), ())), preferred_element_type=F32)


def _dot_exact_lhs(m_bf16, x):
    hi = x.astype(BF16)
    lo = (x - hi.astype(F32)).astype(BF16)
    return _dot(m_bf16, hi) + _dot(m_bf16, lo)


def _dot_exact_rhs(x, m_bf16):
    hi = x.astype(BF16)
    lo = (x - hi.astype(F32)).astype(BF16)
    return _dot(hi, m_bf16) + _dot(lo, m_bf16)


def _sigmoid(x):
    return 1.0 / (1.0 + jnp.exp(-x))


def _softplus(x):
    return jnp.maximum(x, 0.0) + jnp.log(1.0 + jnp.exp(-jnp.abs(x)))


def _rms(x, g):
    return x * lax.rsqrt(jnp.mean(x * x, axis=-1, keepdims=True) + NORM_EPS) * g


def _iota(shape, dim):
    return lax.broadcasted_iota(jnp.int32, shape, dim)


def _chunk_tri(n, strict=False):
    r = _iota((n, n), 0)
    c = _iota((n, n), 1)
    same = (r // CHUNK) == (c // CHUNK)
    return same & ((c < r) if strict else (c <= r))


def _seg_ones(width, seg):
    r = _iota((width, width), 0)
    c = _iota((width, width), 1)
    return jnp.where((r // seg) == (c // seg), 1.0, 0.0).astype(BF16)


def _stack_heads(x):
    lane = _iota((CHUNK, LANES), 1)
    m1 = lane < 64
    top, bot = x[0:CHUNK], x[CHUNK:2 * CHUNK]
    return jnp.concatenate([jnp.where(m1, top, 0.0), jnp.where(m1, 0.0, top),
                            jnp.where(m1, bot, 0.0), jnp.where(m1, 0.0, bot)], axis=0)


def _dup_heads(x):
    top, bot = x[0:CHUNK], x[CHUNK:2 * CHUNK]
    return jnp.concatenate([top, top, bot, bot], axis=0)


def _neumann_inverse(n_mats, eye, to_bd):
    ps = [eye + n for n in n_mats]
    ms = list(n_mats)
    bds = [to_bd(m) for m in ms]
    for _ in range(5):
        ms = [_dot(m, bd) for m, bd in zip(ms, bds)]
        bds = [to_bd(m) for m in ms]
        ps = [p + _dot(p, bd) for p, bd in zip(ps, bds)]
    return ps


def _chunk_rows(x, r0, r1):
    w = x.shape[1]
    return jnp.concatenate([jnp.broadcast_to(x[r0:r0 + 1], (CHUNK, w)),
                            jnp.broadcast_to(x[r1:r1 + 1], (CHUNK, w))], axis=0)


def _with_halo(x, halo, keep):
    return jnp.concatenate([jnp.where(keep, halo, 0.0), x], axis=0)


def _shifted(ext, s):
    if s == 0:
        return ext[HALO:]
    return pltpu.roll(ext, s, 0)[HALO:]


def _params(sem):
    return pltpu.CompilerParams(dimension_semantics=sem, vmem_limit_bytes=VMEM_LIMIT)


def _layer_spec(shape, layer):
    return pl.BlockSpec((None,) + tuple(shape[1:]), lambda *_: (layer, 0, 0), pipeline_mode=pl.Buffered(1))


def _row_spec(tm, width, col=0):
    return pl.BlockSpec((tm, width), lambda i, col=col: (i, col))


def _halo_spec(tm, width, col=0):
    step = tm // HALO
    return pl.BlockSpec((HALO, width), lambda i, col=col: (jnp.maximum(i * step - 1, 0), col))


def _const_spec(shape):
    return pl.BlockSpec(shape, lambda *_: (0,) * len(shape), pipeline_mode=pl.Buffered(1))


def _inproj_body(x_ref, g_ref, w_ref, *o_refs, widths):
    h = _rms(x_ref[...], g_ref[...]).astype(BF16)
    off = 0
    for o_ref, wd in zip(o_refs, widths):
        o_ref[...] = _dot(h, w_ref[:, off:off + wd])
        off += wd


def _inproj(x, g, w_all, layer, widths, tm=512):
    n, d = x.shape
    return pl.pallas_call(
        functools.partial(_inproj_body, widths=widths),
        grid=(n // tm,),
        in_specs=[_row_spec(tm, d), _const_spec((1, d)), _layer_spec(w_all.shape, layer)],
        out_specs=[_row_spec(tm, wd) for wd in widths],
        out_shape=[jax.ShapeDtypeStruct((n, wd), F32) for wd in widths],
        compiler_params=_params(("parallel",)),
        name="inproj",
    )(x, g, w_all)


def _rwkv_body(*refs, has_vres, nb):
    if has_vres:
        (z_ref, zh_ref, l_ref, lh_ref, vf_ref, mu_ref, mul_ref, w0_ref, wup_ref, a0_ref, aup_ref, gup_ref,
         kk_ref, ka_ref, v0_ref, vup_ref, lng_ref, lnb_ref, rk_ref, y_ref, s_ref) = refs
    else:
        (z_ref, zh_ref, l_ref, lh_ref, mu_ref, mul_ref, w0_ref, wup_ref, a0_ref, aup_ref, gup_ref,
         kk_ref, ka_ref, lng_ref, lnb_ref, rk_ref, y_ref, vout_ref, s_ref) = refs

    @pl.when(pl.program_id(1) == 0)
    def _():
        s_ref[...] = jnp.zeros_like(s_ref)

    keep = pl.program_id(1) != 0
    z = z_ref[...]
    za_all = z + mu_ref[...] * (_shifted(_with_halo(z, zh_ref[...], keep), 1) - z)
    zl = l_ref[...]
    zl_all = zl + mul_ref[...] * (_shifted(_with_halo(zl, lh_ref[...], keep), 1) - zl)
    seg = _seg_ones(HALF_W, 64)

    pairs = range(HALF_W // LANES)
    sls = [slice(p * LANES, (p + 1) * LANES) for p in pairs]
    tri = jnp.where(_chunk_tri(ROWS), 1.0, 0.0).astype(BF16)
    rr = _iota((ROWS, 2 * ROWS), 0)
    cc = _iota((ROWS, 2 * ROWS), 1)
    same_chunk = (rr // CHUNK) == (cc // ROWS)
    rc_strict = same_chunk & ((cc % CHUNK) < (rr % CHUNK))
    rc_incl = same_chunk & ((cc % CHUNK) <= (rr % CHUNK))
    eye_rc = jnp.where(same_chunk & ((cc % CHUNK) == (rr % CHUNK)), 1.0, 0.0)
    bd_mask = (_iota((2 * ROWS, 2 * ROWS), 0) // CHUNK) == (_iota((2 * ROWS, 2 * ROWS), 1) // CHUNK)

    def to_bd(m_rc):
        return jnp.where(bd_mask, _dup_heads(m_rc), 0.0).astype(BF16)
    q, v_s, a_in_s, r_in, bk_t, e_col, post_in = {}, {}, {}, {}, {}, {}, {}
    a_rb, av, tt, r_eff, y_off, mb = {}, {}, {}, {}, {}, {}
    zero_blk = jnp.zeros((LANES, LANES), BF16)

    def prepare(blk):
        rows = slice(blk * ROWS, (blk + 1) * ROWS)
        za = za_all[rows]
        zl = zl_all[rows]
        r = za[:, 0:HALF_W]
        k_raw = za[:, HALF_W:2 * HALF_W]
        v = za[:, 2 * HALF_W:3 * HALF_W]
        if has_vres:
            gate = _sigmoid(v0_ref[...] + _dot(zl, vup_ref[...]))
            v = v + (vf_ref[rows, :] - v) * gate
        else:
            vout_ref[rows, :] = v
        w_log = -_softplus(-(w0_ref[...] + _dot(jnp.tanh(zl), wup_ref[...]))) - 0.5
        lw = -jnp.exp(w_log)
        a = _sigmoid(a0_ref[...] + _dot(zl, aup_ref[...]))
        g = _dot(_sigmoid(zl), gup_ref[...])
        kkk = k_raw * kk_ref[...]
        kk = kkk * lax.rsqrt(_dot(kkk * kkk, seg) + L2_EPS)
        k = k_raw * (1.0 + (a - 1.0) * ka_ref[...])
        bonus = _dot_exact_rhs(r * k * rk_ref[...], seg)
        post_in[blk] = (bonus * v, g)

        b = _dot_exact_lhs(tri, lw)
        bref = _chunk_rows(b, CHUNK // 2, CHUNK + CHUNK // 2)
        blast = _chunk_rows(b, CHUNK - 1, 2 * CHUNK - 1)
        e_pos = jnp.exp(b - bref)
        e_neg = jnp.exp(bref - b)
        e_in = jnp.exp(b)
        e_st = jnp.exp(blast - b)
        e_last = jnp.exp(blast)
        kka = kk * a
        neg_kk_exc = -(kk * jnp.exp(-lw))
        a_t = neg_kk_exc * e_pos
        a_in = neg_kk_exc * e_in
        r_t = r * e_pos
        r_in_blk = r * e_in
        b_t = kka * e_neg
        b_st = kka * e_st
        k_t = k * e_neg
        k_st = k * e_st
        for p in pairs:
            sl = sls[p]
            un = (blk, p)
            q[un] = _dot_nt(jnp.concatenate([a_t[:, sl], r_t[:, sl]], axis=0),
                            jnp.concatenate([_stack_heads(b_t[:, sl]), _stack_heads(k_t[:, sl])], axis=0))
            v_s[un] = _stack_heads(v[:, sl]).astype(BF16)
            a_in_s[un] = _stack_heads(a_in[:, sl])
            r_in[un] = r_in_blk[:, sl]
            b_s = _stack_heads(b_st[:, sl])
            k_s = _stack_heads(k_st[:, sl])
            for c in range(2):
                ss = slice(c * LANES, (c + 1) * LANES)
                bk_t[un + (c,)] = jnp.concatenate([b_s[ss].T, k_s[ss].T], axis=1).astype(BF16)
                e_col[un + (c,)] = jnp.broadcast_to(e_last[c * CHUNK:c * CHUNK + 1, sl], (LANES, LANES)).T
    def solve(units):
        a_kk = {}
        for u in units:
            a_rb[u] = jnp.where(rc_incl, q[u][ROWS:2 * ROWS, 0:256], 0.0).astype(BF16)
            a_kk[u] = jnp.concatenate([jnp.where(rc_strict, q[u][0:ROWS, 256:512], 0.0),
                                       jnp.where(rc_incl, q[u][ROWS:2 * ROWS, 256:512], 0.0)], axis=0)
        t_list = _neumann_inverse([jnp.where(rc_strict, q[u][0:ROWS, 0:256], 0.0) for u in units], eye_rc, to_bd)
        for u in units:
            av[u] = _dot(a_kk[u], v_s[u])
        for u, t_inv in zip(units, t_list):
            tt[u] = _dot(t_inv, jnp.concatenate([a_in_s[u], _stack_heads(av[u][0:ROWS])], axis=1))
        wu_s = {u: jnp.concatenate([_stack_heads(tt[u][:, 0:LANES]), _stack_heads(tt[u][:, LANES:2 * LANES])],
                                   axis=1).astype(BF16) for u in units}
        for u in units:
            ru = _dot(a_rb[u], wu_s[u])
            r_eff[u] = (r_in[u] + ru[:, 0:LANES]).astype(BF16)
            y_off[u] = ru[:, LANES:2 * LANES] + av[u][ROWS:2 * ROWS]
        for c in range(2):
            ss = slice(c * LANES, (c + 1) * LANES)
            for u in units:
                rhs = jnp.concatenate([wu_s[u][ss], jnp.concatenate([zero_blk, v_s[u][ss]], axis=1)], axis=0)
                mb[u + (c,)] = _dot(bk_t[u + (c,)], rhs)

    state = [s_ref[p] for p in pairs]

    def sequential(blk):
        rows = slice(blk * ROWS, (blk + 1) * ROWS)
        for c in range(2):
            rs = slice(c * CHUNK, (c + 1) * CHUNK)
            out_rows = slice(blk * ROWS + c * CHUNK, blk * ROWS + (c + 1) * CHUNK)
            for p in pairs:
                un = (blk, p)
                h_b = state[p].astype(BF16)
                y_ref[out_rows, sls[p]] = _dot(r_eff[un][rs], h_b) + y_off[un][rs]
                m_b = mb[un + (c,)]
                state[p] = state[p] * e_col[un + (c,)] + _dot(m_b[:, 0:LANES], h_b) + m_b[:, LANES:2 * LANES]
        y = y_ref[rows, :]
        mu_y = _dot_exact_rhs(y, seg) * (1.0 / 64)
        d = y - mu_y
        var = _dot(d * d, seg) * (1.0 / 64)
        bonus_v, g = post_in[blk]
        y_ref[rows, :] = (d * lax.rsqrt(var + RWKV_LN_EPS) * lng_ref[...] + lnb_ref[...] + bonus_v) * g

    for blk in range(nb):
        prepare(blk)
    solve([(blk, p) for blk in range(nb) for p in pairs])
    for blk in range(nb):
        sequential(blk)
    for p in pairs:
        s_ref[p] = state[p]


def _rwkv(z_rkv, z_lr, v_first, prm, ln_g, ln_b, r_k, batch, seq_len, nb=4):
    n = z_rkv.shape[0]
    has_vres = v_first is not None
    rows = nb * ROWS
    nblk = seq_len // rows
    w3 = 3 * HALF_W

    def tile(width):
        return pl.BlockSpec((rows, width), lambda b, j: (b * nblk + j, 0))

    def halo(width):
        step = rows // HALO
        return pl.BlockSpec((HALO, width), lambda b, j: (jnp.maximum((b * nblk + j) * step - 1, 0), 0))

    ins = [z_rkv, z_rkv, z_lr, z_lr]
    specs = [tile(w3), halo(w3), tile(A_LR_W), halo(A_LR_W)]
    if has_vres:
        ins.append(v_first)
        specs.append(tile(HALF_W))
    names = ["mu", "mul", "w0", "wup", "a0", "aup", "gup", "k_k", "k_a"] + (["v0", "vup"] if has_vres else [])
    for nm in names:
        ins.append(prm[nm])
        specs.append(_const_spec(prm[nm].shape))
    ins += [ln_g, ln_b, r_k]
    specs += [_const_spec((1, HALF_W))] * 3
    n_out = 1 if has_vres else 2
    out = pl.pallas_call(
        functools.partial(_rwkv_body, has_vres=has_vres, nb=nb),
        grid=(batch, nblk),
        in_specs=specs,
        out_specs=[tile(HALF_W)] * n_out,
        out_shape=[jax.ShapeDtypeStruct((n, HALF_W), F32)] * n_out,
        scratch_shapes=[pltpu.VMEM((HALF_W // LANES, LANES, LANES), F32)],
        compiler_params=_params(("parallel", "arbitrary")),
        name="rwkv",
    )(*ins)
    return (out[0], None) if has_vres else (out[0], out[1])


def _gelu_erf(x):
    return 0.5 * x * (1.0 + lax.erf(x * (2.0 ** -0.5)))


def _sgu_body(u_ref, v_ref, lng_ref, lnb_ref, w_ref, bias_ref, o_ref, *, tm):
    seg = _seg_ones(HALF_W, 64)
    u = _gelu_erf(u_ref[...])
    v = _gelu_erf(v_ref[...])
    mu = _dot_exact_rhs(v, seg) * (1.0 / 64)
    d = v - mu
    var = _dot(d * d, seg) * (1.0 / 64)
    vn = d * lax.rsqrt(var + SGU_LN_EPS) * lng_ref[...] + lnb_ref[...]
    r = _iota((SGU_BLOCK, 2 * SGU_BLOCK), 0)
    c = _iota((SGU_BLOCK, 2 * SGU_BLOCK), 1)
    causal = (r // CHUNK) >= ((c % SGU_BLOCK) // CHUNK)
    lane = _iota((SGU_BLOCK, LANES), 1)
    m1 = lane < 64
    bias = bias_ref[...]
    for p in range(HALF_W // LANES):
        sl = slice(p * LANES, (p + 1) * LANES)
        w = jnp.where(causal, w_ref[p], 0.0)
        for nb in range(tm // SGU_BLOCK):
            rs = slice(nb * SGU_BLOCK, (nb + 1) * SGU_BLOCK)
            blk = vn[rs, sl]
            stacked = jnp.concatenate([jnp.where(m1, blk, 0.0), jnp.where(m1, 0.0, blk)], axis=0)
            mixed = _dot(w, stacked) + bias[:, sl]
            o_ref[rs, sl] = u[rs, sl] * mixed


def _sgu(z_b, ln_g, ln_b, w_pairs, bias, tm=512):
    n = z_b.shape[0]
    return pl.pallas_call(
        functools.partial(_sgu_body, tm=tm),
        grid=(n // tm,),
        in_specs=[_row_spec(tm, HALF_W, 0), _row_spec(tm, HALF_W, 1),
                  _const_spec((1, HALF_W)), _const_spec((1, HALF_W)),
                  _const_spec(w_pairs.shape), _const_spec(bias.shape)],
        out_specs=_row_spec(tm, HALF_W),
        out_shape=jax.ShapeDtypeStruct((n, HALF_W), F32),
        compiler_params=_params(("parallel",)),
        name="sgu",
    )(z_b, z_b, ln_g, ln_b, w_pairs, bias)


def _hgrn_body(q_ref, f_ref, i_ref, gt_ref, lbl_ref, ng_ref, y_ref, s_ref, *, layer, nb):
    @pl.when(pl.program_id(1) == 0)
    def _():
        s_ref[...] = jnp.zeros_like(s_ref)

    logits = lbl_ref[...]
    e = jnp.exp(logits - jnp.max(logits, axis=0, keepdims=True))
    prob = e / jnp.sum(e, axis=0, keepdims=True)
    lb = jnp.sum(prob[0:layer + 1], axis=0, keepdims=True) - prob[0:1]

    f = lb + (1.0 - lb) * _sigmoid(f_ref[...])
    lf_all = jnp.log(f)
    kx_all = 1.0 - f
    q = q_ref[...]
    qs_all = q * _sigmoid(q)
    tri = jnp.where(_chunk_tri(ROWS), 1.0, 0.0).astype(BF16)
    rr = _iota((2 * ROWS, ROWS), 0)
    cc = _iota((2 * ROWS, ROWS), 1)
    valid = ((rr // ROWS) == (cc // CHUNK)) & ((cc % CHUNK) <= (rr % CHUNK))
    m1 = _iota((CHUNK, LANES), 1) < 64
    blockmask = (_iota((LANES, LANES), 0) // 64) == (_iota((LANES, LANES), 1) // 64)
    pairs = range(HALF_W // LANES)
    state = [s_ref[p] for p in pairs]

    for blk in range(nb):
        rows = slice(blk * ROWS, (blk + 1) * ROWS)
        qs = qs_all[rows]
        kx = kx_all[rows]
        v = i_ref[rows, :]
        b = _dot_exact_lhs(tri, lf_all[rows])
        bref = _chunk_rows(b, CHUNK // 2, CHUNK + CHUNK // 2)
        blast = _chunk_rows(b, CHUNK - 1, 2 * CHUNK - 1)
        q_t = qs * jnp.exp(b - bref)
        k_t = kx * jnp.exp(bref - b)
        q_in = qs * jnp.exp(b)
        k_st = kx * jnp.exp(blast - b)
        e_last = jnp.exp(blast)
        for p in pairs:
            sl = slice(p * LANES, (p + 1) * LANES)
            sc = _dot_nt(_stack_heads(q_t[:, sl]), k_t[:, sl])
            oi = _dot(jnp.where(valid, sc, 0.0), v[:, sl])
            for c in range(2):
                rs = slice(c * CHUNK, (c + 1) * CHUNK)
                o_inter = _dot_nt(q_in[rs, sl], state[p])
                o_intra = (jnp.where(m1, oi[c * 128:c * 128 + 64], 0.0)
                           + jnp.where(m1, 0.0, oi[c * 128 + 64:c * 128 + 128]))
                y_ref[blk * ROWS + c * CHUNK:blk * ROWS + (c + 1) * CHUNK, sl] = o_inter + o_intra
                upd = jnp.where(blockmask, _dot_tn(v[rs, sl], k_st[rs, sl]), 0.0)
                state[p] = state[p] * e_last[c * CHUNK:c * CHUNK + 1, sl] + upd
    for p in pairs:
        s_ref[p] = state[p]

    o = y_ref[...]
    ms = _dot(o * o, _seg_ones(HALF_W, 64)) * (1.0 / 64)
    gate = gt_ref[...]
    y_ref[...] = o * lax.rsqrt(ms + NORM_EPS) * ng_ref[...] * (gate * _sigmoid(gate))


def _hgrn(z_c, lb_logits, norm_g, layer, batch, seq_len, nb=4):
    n = z_c.shape[0]
    rows = nb * ROWS
    nblk = seq_len // rows

    def spec(col):
        return pl.BlockSpec((rows, HALF_W), lambda b, j: (b * nblk + j, col))

    return pl.pallas_call(
        functools.partial(_hgrn_body, layer=layer, nb=nb),
        grid=(batch, nblk),
        in_specs=[spec(c) for c in range(4)] + [_const_spec(lb_logits.shape), _const_spec((1, HALF_W))],
        out_specs=spec(0),
        out_shape=jax.ShapeDtypeStruct((n, HALF_W), F32),
        scratch_shapes=[pltpu.VMEM((HALF_W // LANES, LANES, LANES), F32)],
        compiler_params=_params(("parallel", "arbitrary")),
        name="hgrn",
    )(z_c, z_c, z_c, z_c, lb_logits, norm_g)


def _gdn_prep_body(z_ref, zh_ref, ba_ref, cw_ref, alog_ref, dtb_ref, eb_ref, ea_ref,
                   q_o, k_o, v_o, beta_o, g_o, *, tiles_per_batch):
    keep = (pl.program_id(0) % tiles_per_batch) != 0
    ext = _with_halo(z_ref[...], zh_ref[...], keep)
    cw = cw_ref[...]
    acc = _shifted(ext, 0) * cw[D_CONV - 1:D_CONV]
    for j in range(D_CONV - 1):
        acc = acc + _shifted(ext, D_CONV - 1 - j) * cw[j:j + 1]
    qkv = acc * _sigmoid(acc)
    seg = _seg_ones(HALF_W, D_HD)
    q = qkv[:, 0:HALF_W]
    k = qkv[:, HALF_W:2 * HALF_W]
    q_o[...] = q * lax.rsqrt(_dot(q * q, seg) + L2_EPS) * (D_HD ** -0.5)
    k_o[...] = k * lax.rsqrt(_dot(k * k, seg) + L2_EPS)
    v_o[...] = qkv[:, 2 * HALF_W:3 * HALF_W]
    ba = ba_ref[...]
    beta_o[...] = _sigmoid(_dot_exact_rhs(ba, eb_ref[...]))
    a_full = _dot_exact_rhs(ba, ea_ref[...])
    g_o[...] = -jnp.exp(alog_ref[...]) * _softplus(a_full + dtb_ref[...])


def _gdn_prep(z_qkv, z_ba, conv_w, a_log_full, dt_bias_full, e_b, e_a, seq_len, tm=512):
    n = z_qkv.shape[0]
    w3 = 3 * HALF_W
    return pl.pallas_call(
        functools.partial(_gdn_prep_body, tiles_per_batch=seq_len // tm),
        grid=(n // tm,),
        in_specs=[_row_spec(tm, w3), _halo_spec(tm, w3), _row_spec(tm, LANES),
                  _const_spec((D_CONV, w3)), _const_spec((1, HALF_W)), _const_spec((1, HALF_W)),
                  _const_spec((LANES, HALF_W)), _const_spec((LANES, HALF_W))],
        out_specs=[_row_spec(tm, HALF_W)] * 5,
        out_shape=[jax.ShapeDtypeStruct((n, HALF_W), F32)] * 5,
        compiler_params=_params(("parallel",)),
        name="gdn_prep",
    )(z_qkv, z_qkv, z_ba, conv_w, a_log_full, dt_bias_full, e_b, e_a)


def _gdn_rec_body(q_ref, k_ref, v_ref, beta_ref, g_ref, z_ref, ng_ref, y_ref, s_ref, *, nb):
    @pl.when(pl.program_id(1) == 0)
    def _():
        s_ref[...] = jnp.zeros_like(s_ref)

    tri = jnp.where(_chunk_tri(ROWS), 1.0, 0.0).astype(BF16)
    incl = _chunk_tri(ROWS)
    strict = _chunk_tri(ROWS, strict=True)
    heads = range(D_HEADS)
    sls = [slice(h * D_HD, (h + 1) * D_HD) for h in heads]
    eye = jnp.where(_iota((ROWS, ROWS), 0) == _iota((ROWS, ROWS), 1), 1.0, 0.0)
    zero_blk = jnp.zeros((ROWS, D_HD), BF16)

    def to_bd(m_rc):
        mb = m_rc.astype(BF16)
        return jnp.concatenate([jnp.concatenate([mb[:, 0:D_HD], zero_blk], axis=1),
                                jnp.concatenate([zero_blk, mb[:, D_HD:2 * D_HD]], axis=1)], axis=0)

    units = [(blk, h) for blk in range(nb) for h in heads]
    neg_lower, qk, rhs, wq, k_st_t, d_last = {}, {}, {}, {}, {}, {}
    for blk in range(nb):
        rows = slice(blk * ROWS, (blk + 1) * ROWS)
        gc_all = _dot_exact_lhs(tri, g_ref[rows, :])
        glast_all = _chunk_rows(gc_all, CHUNK - 1, 2 * CHUNK - 1)
        for h in heads:
            un = (blk, h)
            gc = gc_all[:, sls[h]]
            glast = glast_all[:, sls[h]]
            diff = gc - gc.T
            decay = jnp.where(incl, jnp.exp(jnp.where(incl, diff, 0.0)), 0.0)
            q = q_ref[rows, sls[h]]
            k = k_ref[rows, sls[h]]
            beta = beta_ref[rows, sls[h]]
            kb = k * beta
            kq = _dot_nt(jnp.concatenate([kb, q], axis=0), k)
            neg_lower[un] = -jnp.where(strict, kq[0:ROWS] * decay, 0.0)
            qk[un] = (kq[ROWS:2 * ROWS] * decay).astype(BF16)
            egc = jnp.exp(gc)
            rhs[un] = jnp.concatenate([v_ref[rows, sls[h]] * beta, kb * egc], axis=1)
            wq[un] = q * egc
            k_st_t[un] = (k * jnp.exp(glast - gc)).T.astype(BF16)
            d_last[un] = jnp.exp(glast)
    pair_units = [(blk, h) for blk in range(nb) for h in range(0, D_HEADS, 2)]
    t_pairs = _neumann_inverse([jnp.concatenate([neg_lower[(blk, h)], neg_lower[(blk, h + 1)]], axis=1)
                                for blk, h in pair_units], jnp.concatenate([eye, eye], axis=1), to_bd)
    t_pairs = dict(zip(pair_units, t_pairs))
    sol = {(blk, h): _dot(t_pairs[(blk, h - h % 2)][:, (h % 2) * D_HD:(h % 2 + 1) * D_HD], rhs[(blk, h)])
           for blk, h in units}
    qo = {un: _dot(qk[un], sol[un]) for un in units}
    q_eff = {un: (wq[un] - qo[un][:, D_HD:2 * D_HD]).astype(BF16) for un in units}
    chunk_row = _iota((ROWS, 2 * D_HD), 0) // CHUNK
    kuw = {}
    for c in range(2):
        for un in units:
            kuw[un + (c,)] = _dot(k_st_t[un], jnp.where(chunk_row == c, sol[un], 0.0))
    state = [s_ref[h] for h in heads]
    for blk in range(nb):
        for c in range(2):
            rs = slice(c * CHUNK, (c + 1) * CHUNK)
            out_rows = slice(blk * ROWS + c * CHUNK, blk * ROWS + (c + 1) * CHUNK)
            for h in heads:
                un = (blk, h)
                s_b = state[h].astype(BF16)
                o = _dot(q_eff[un][rs], s_b) + qo[un][rs, 0:D_HD]
                k_uw = kuw[un + (c,)]
                state[h] = (state[h] * d_last[un][c * CHUNK:c * CHUNK + 1] + k_uw[:, 0:D_HD]
                            - _dot(k_uw[:, D_HD:2 * D_HD], s_b))
                z = z_ref[out_rows, sls[h]]
                y_ref[out_rows, sls[h]] = _rms(o, ng_ref[...]) * (z * _sigmoid(z))
    for h in heads:
        s_ref[h] = state[h]


def _gdn_rec(q, k, v, beta, g, z, norm_g, batch, seq_len, nb=4):
    n = q.shape[0]
    rows = nb * ROWS
    nblk = seq_len // rows
    spec = pl.BlockSpec((rows, HALF_W), lambda b, j: (b * nblk + j, 0))
    return pl.pallas_call(
        functools.partial(_gdn_rec_body, nb=nb),
        grid=(batch, nblk),
        in_specs=[spec] * 6 + [_const_spec((1, D_HD))],
        out_specs=spec,
        out_shape=jax.ShapeDtypeStruct((n, HALF_W), F32),
        scratch_shapes=[pltpu.VMEM((D_HEADS, D_HD, D_HD), F32)],
        compiler_params=_params(("parallel", "arbitrary")),
        name="gdn_rec",
    )(q, k, v, beta, g, z, norm_g)


def _gelu_tanh(x):
    return 0.5 * x * (1.0 + jnp.tanh((2.0 / jnp.pi) ** 0.5 * (x + 0.044715 * (x * x * x))))


def _mix_ffn_body(x_ref, xh_ref, ya_ref, yah_ref, yb_ref, ybh_ref, wo_ref, gmix_ref, gpre_ref,
                  win_ref, cw_ref, cb_ref, wout_ref, gpost_ref, o_ref, *, tiles_per_batch):
    keep = (pl.program_id(0) % tiles_per_batch) != 0
    xe = jnp.concatenate([xh_ref[...], x_ref[...]], axis=0)
    yae = jnp.concatenate([yah_ref[...], ya_ref[...]], axis=0)
    ybe = jnp.concatenate([ybh_ref[...], yb_ref[...]], axis=0)
    mix = _dot(yae, wo_ref[0:HALF_W, :]) + _dot(ybe, wo_ref[HALF_W:2 * HALF_W, :])
    x1 = xe + _rms(mix, gmix_ref[...])
    h = _rms(x1, gpre_ref[...]).astype(BF16)
    cw = cw_ref[...]
    cb = cb_ref[...]
    ffw = D_FF // FF_SPLIT
    y = None
    for j in range(FF_SPLIT):
        cols = slice(j * ffw, (j + 1) * ffw)
        gate = _dot(h, win_ref[:, cols])
        ext = jnp.concatenate([jnp.where(keep, gate[0:HALO], 0.0), gate[HALO:]], axis=0)
        up = _dot(h[HALO:], win_ref[:, D_FF + j * ffw:D_FF + (j + 1) * ffw])
        conv = (ext[HALO:] * cw[2:3, cols] + _shifted(ext, 1) * cw[1:2, cols]
                + _shifted(ext, 2) * cw[0:1, cols] + cb[:, cols])
        part = _dot(_gelu_tanh(conv) * up, wout_ref[cols, :])
        y = part if y is None else y + part
    o_ref[...] = x1[HALO:] + _rms(y, gpost_ref[...])


def _mix_ffn(x, ya, yb, w_o_all, mix_layer, g_mix, g_pre, w_in_all, conv_w, conv_b, w_out_all, g_post, layer,
             seq_len, tm=512):
    n, d = x.shape
    return pl.pallas_call(
        functools.partial(_mix_ffn_body, tiles_per_batch=seq_len // tm),
        grid=(n // tm,),
        in_specs=[_row_spec(tm, d), _halo_spec(tm, d),
                  _row_spec(tm, HALF_W), _halo_spec(tm, HALF_W), _row_spec(tm, HALF_W), _halo_spec(tm, HALF_W),
                  _layer_spec(w_o_all.shape, mix_layer), _const_spec((1, d)), _const_spec((1, d)),
                  _layer_spec(w_in_all.shape, layer), _const_spec((3, D_FF)), _const_spec((1, D_FF)),
                  _layer_spec(w_out_all.shape, layer), _const_spec((1, d))],
        out_specs=_row_spec(tm, d),
        out_shape=jax.ShapeDtypeStruct((n, d), F32),
        compiler_params=_params(("parallel",)),
        name="mix_ffn",
    )(x, x, ya, ya, yb, yb, w_o_all, g_mix, g_pre, w_in_all, conv_w, conv_b, w_out_all, g_post)


def _pad_rows(w, start, total):
    return jnp.zeros((total, w.shape[1]), w.dtype).at[start:start + w.shape[0]].set(w)


def _even_in_weights(ev_w_in, rwkv_vres_down):
    a_cols = 3 * HALF_W + A_DECAY_R + A_AAA_R + A_GATE_R
    n_even, d, _ = ev_w_in.shape
    w = ev_w_in.astype(BF16)
    vres = jnp.concatenate([jnp.zeros((1, d, A_MV_R), BF16), rwkv_vres_down.astype(BF16)], axis=0)
    pad = jnp.zeros((n_even, d, A_LR_W - (a_cols - 3 * HALF_W) - A_MV_R), BF16)
    return jnp.concatenate([w[:, :, 0:a_cols], vres, pad, w[:, :, a_cols:]], axis=2)


def _even_params(e, rwkv_mu, rwkv_w0, rwkv_w_up, rwkv_a0, rwkv_a_up, rwkv_g_up, rwkv_k_k, rwkv_k_a,
                 rwkv_vres_up, rwkv_v0):
    lr_w = A_DECAY_R + A_AAA_R + A_GATE_R
    mu = rwkv_mu[e]
    o_w, o_a, o_g, o_v = 0, A_DECAY_R, A_DECAY_R + A_AAA_R, lr_w
    prm = {
        "mu": mu[None, 0:3 * HALF_W],
        "mul": jnp.concatenate([mu[3 * HALF_W:], jnp.zeros((A_LR_W - lr_w,), F32)])[None],
        "w0": rwkv_w0[e][None],
        "wup": _pad_rows(rwkv_w_up[e], o_w, A_LR_W),
        "a0": rwkv_a0[e][None],
        "aup": _pad_rows(rwkv_a_up[e], o_a, A_LR_W),
        "gup": _pad_rows(rwkv_g_up[e], o_g, A_LR_W),
        "k_k": rwkv_k_k[e][None],
        "k_a": rwkv_k_a[e][None],
    }
    if e > 0:
        prm["v0"] = rwkv_v0[e - 1][None]
        prm["vup"] = _pad_rows(rwkv_vres_up[e - 1], o_v, A_LR_W)
    return prm


def _odd_in_weights(od_w_in):
    return jnp.pad(od_w_in.astype(BF16), ((0, 0), (0, 0), (0, LANES - 2 * D_HEADS)))


def _head_expand(offset):
    r = jnp.arange(LANES)[:, None]
    c = jnp.arange(HALF_W)[None, :]
    return (r == offset + c // D_HD).astype(BF16)


def kernel(x, norm_mix_pre, norm_mix_post, norm_ffn_pre, norm_ffn_post, ev_w_in, ev_w_out, rwkv_mu, rwkv_w0, rwkv_w_up, rwkv_a0, rwkv_a_up, rwkv_g_up, rwkv_k_k, rwkv_k_a, rwkv_r_k, rwkv_ln_g, rwkv_ln_b, rwkv_vres_down, rwkv_vres_up, rwkv_v0, sgu_ln_g, sgu_ln_b, sgu_w, sgu_b, od_w_in, od_w_out, hgrn_lb_logits, hgrn_norm_g, gdn_conv_w, gdn_a_log, gdn_dt_bias, gdn_norm_g, ffn_w_in, ffn_conv_w, ffn_conv_b, ffn_w_out):
    batch, seq_len, d = x.shape
    depth = norm_mix_pre.shape[0]
    n = batch * seq_len
    xf = x.reshape(n, d)
    ev_w_in_p = _even_in_weights(ev_w_in, rwkv_vres_down)
    od_w_in_p = _odd_in_weights(od_w_in)
    ev_w_out_b = ev_w_out.astype(BF16)
    od_w_out_b = od_w_out.astype(BF16)
    ffn_w_in_b = ffn_w_in.astype(BF16)
    ffn_w_out_b = ffn_w_out.astype(BF16)
    v_first = None
    for l in range(depth):
        if l % 2 == 0:
            e = l // 2
            prm = _even_params(e, rwkv_mu, rwkv_w0, rwkv_w_up, rwkv_a0, rwkv_a_up, rwkv_g_up, rwkv_k_k, rwkv_k_a,
                               rwkv_vres_up, rwkv_v0)
            z_rkv, z_lr, z_b = _inproj(xf, norm_mix_pre[l][None], ev_w_in_p, e, (3 * HALF_W, A_LR_W, 2 * HALF_W))
            ya, v_a = _rwkv(z_rkv, z_lr, v_first if e > 0 else None, prm, rwkv_ln_g[e][None], rwkv_ln_b[e][None],
                            rwkv_r_k[e].reshape(1, HALF_W), batch, seq_len)
            if e == 0:
                v_first = v_a
            w_s = sgu_w[e]
            w_pairs = jnp.concatenate([w_s[0::2], w_s[1::2]], axis=2)
            bias = jnp.repeat(sgu_b[e].T, HALF_W // B_GROUPS, axis=1)
            yb = _sgu(z_b, sgu_ln_g[e][None], sgu_ln_b[e][None], w_pairs, bias)
            w_out_all, mix_layer = ev_w_out_b, e
        else:
            o = l // 2
            z_c, z_qkv, z_z, z_ba = _inproj(xf, norm_mix_pre[l][None], od_w_in_p, o,
                                            (4 * HALF_W, 3 * HALF_W, HALF_W, LANES))
            ya = _hgrn(z_c, hgrn_lb_logits, hgrn_norm_g[o][None], o, batch, seq_len)
            q, k, v, beta, g = _gdn_prep(z_qkv, z_ba, gdn_conv_w[o],
                                         jnp.repeat(gdn_a_log[o], D_HD)[None],
                                         jnp.repeat(gdn_dt_bias[o], D_HD)[None],
                                         _head_expand(0), _head_expand(D_HEADS), seq_len)
            yb = _gdn_rec(q, k, v, beta, g, z_z, gdn_norm_g[o][None], batch, seq_len)
            w_out_all, mix_layer = od_w_out_b, o
        xf = _mix_ffn(xf, ya, yb, w_out_all, mix_layer, norm_mix_post[l][None], norm_ffn_pre[l][None],
                      ffn_w_in_b, ffn_conv_w[l], ffn_conv_b[l][None], ffn_w_out_b, norm_ffn_post[l][None], l,
                      seq_len)
    return xf.reshape(batch, seq_len, d)
```

```python
import functools

import jax
import jax.numpy as jnp
from jax import lax
from jax.experimental import pallas as pl
from jax.experimental.pallas import tpu as pltpu

F32 = jnp.float32
BF16 = jnp.bfloat16

D_MODEL = 1024
CHUNK = 64
HALF_W = 512
A_DECAY_R = 32
A_AAA_R = 32
A_MV_R = 32
A_GATE_R = 96
A_LR_W = 256
SGU_BLOCK = 128
B_GROUPS = 8
D_HEADS = 4
D_HD = 128
D_CONV = 4
D_FF = 2816
NORM_EPS = 1e-6
RWKV_LN_EPS = 64e-5
SGU_LN_EPS = 1e-5
L2_EPS = 1e-6

ROWS = 128
HALO = 16
FF_SPLIT = 2
LANES = 128
VMEM_LIMIT = 56 * 1024 * 1024


def _dot(a, b):
    return jnp.dot(a.astype(BF16), b.astype(BF16), preferred_element_type=F32)


def _dot_nt(a, b):
    return lax.dot_general(a.astype(BF16), b.astype(BF16), (((1,), (1,)), ((), ())), preferred_element_type=F32)


def _dot_tn(a, b):
    return lax.dot_general(a.astype(BF16), b.astype(BF16), (((0,), (0,)), ((), ())), preferred_element_type=F32)


def _dot_exact_lhs(m_bf16, x):
    hi = x.astype(BF16)
    lo = (x - hi.astype(F32)).astype(BF16)
    return _dot(m_bf16, hi) + _dot(m_bf16, lo)


def _dot_exact_rhs(x, m_bf16):
    hi = x.astype(BF16)
    lo = (x - hi.astype(F32)).astype(BF16)
    return _dot(hi, m_bf16) + _dot(lo, m_bf16)


def _sigmoid(x):
    return 1.0 / (1.0 + jnp.exp(-x))


def _softplus(x):
    return jnp.maximum(x, 0.0) + jnp.log(1.0 + jnp.exp(-jnp.abs(x)))


def _rms(x, g):
    return x * lax.rsqrt(jnp.mean(x * x, axis=-1, keepdims=True) + NORM_EPS) * g


def _iota(shape, dim):
    return lax.broadcasted_iota(jnp.int32, shape, dim)


def _chunk_tri(n, strict=False):
    r = _iota((n, n), 0)
    c = _iota((n, n), 1)
    same = (r // CHUNK) == (c // CHUNK)
    return same & ((c < r) if strict else (c <= r))


def _seg_ones(width, seg):
    r = _iota((width, width), 0)
    c = _iota((width, width), 1)
    return jnp.where((r // seg) == (c // seg), 1.0, 0.0).astype(BF16)


def _stack_heads(x):
    lane = _iota((CHUNK, LANES), 1)
    m1 = lane < 64
    top, bot = x[0:CHUNK], x[CHUNK:2 * CHUNK]
    return jnp.concatenate([jnp.where(m1, top, 0.0), jnp.where(m1, 0.0, top),
                            jnp.where(m1, bot, 0.0), jnp.where(m1, 0.0, bot)], axis=0)


def _neumann_steps(n_mats, eye, to_bd, out):
    ps = [eye + n for n in n_mats]
    ms = list(n_mats)
    bds = [to_bd(m) for m in ms]
    for _ in range(5):
        ms = [_dot(m, bd) for m, bd in zip(ms, bds)]
        yield
        bds = [to_bd(m) for m in ms]
        ps = [p + _dot(p, bd) for p, bd in zip(ps, bds)]
        yield
    out.extend(ps)


def _round_robin(*gens):
    gens = list(gens)
    while gens:
        for g in list(gens):
            try:
                next(g)
            except StopIteration:
                gens.remove(g)


def _chunk_rows(x, r0, r1):
    w = x.shape[1]
    return jnp.concatenate([jnp.broadcast_to(x[r0:r0 + 1], (CHUNK, w)),
                            jnp.broadcast_to(x[r1:r1 + 1], (CHUNK, w))], axis=0)


def _with_halo(x, halo, keep):
    return jnp.concatenate([jnp.where(keep, halo, 0.0), x], axis=0)


def _shifted(ext, s):
    if s == 0:
        return ext[HALO:]
    return pltpu.roll(ext, s, 0)[HALO:]


def _params(sem):
    return pltpu.CompilerParams(dimension_semantics=sem, vmem_limit_bytes=VMEM_LIMIT)


def _layer_spec(shape, layer):
    return pl.BlockSpec((None,) + tuple(shape[1:]), lambda *_: (layer, 0, 0), pipeline_mode=pl.Buffered(1))


def _row_spec(tm, width, col=0):
    return pl.BlockSpec((tm, width), lambda i, col=col: (i, col))


def _halo_spec(tm, width, col=0):
    step = tm // HALO
    return pl.BlockSpec((HALO, width), lambda i, col=col: (jnp.maximum(i * step - 1, 0), col))


def _const_spec(shape):
    return pl.BlockSpec(shape, lambda *_: (0,) * len(shape), pipeline_mode=pl.Buffered(1))


def _inproj_body(x_ref, g_ref, w_ref, *o_refs, widths):
    h = _rms(x_ref[...], g_ref[...]).astype(BF16)
    off = 0
    for o_ref, wd in zip(o_refs, widths):
        o_ref[...] = _dot(h, w_ref[:, off:off + wd])
        off += wd


def _inproj(x, g, w_all, layer, widths, tm=512):
    n, d = x.shape
    return pl.pallas_call(
        functools.partial(_inproj_body, widths=widths),
        grid=(n // tm,),
        in_specs=[_row_spec(tm, d), _const_spec((1, d)), _layer_spec(w_all.shape, layer)],
        out_specs=[_row_spec(tm, wd) for wd in widths],
        out_shape=[jax.ShapeDtypeStruct((n, wd), F32) for wd in widths],
        compiler_params=_params(("parallel",)),
        name="inproj",
    )(x, g, w_all)


def _rwkv_body(*refs, has_vres, nb):
    if has_vres:
        (z_ref, zh_ref, l_ref, lh_ref, vf_ref, mu_ref, mul_ref, w0_ref, wup_ref, a0_ref, aup_ref, gup_ref,
         kk_ref, ka_ref, v0_ref, vup_ref, lng_ref, lnb_ref, rk_ref, y_ref, s_ref) = refs
    else:
        (z_ref, zh_ref, l_ref, lh_ref, mu_ref, mul_ref, w0_ref, wup_ref, a0_ref, aup_ref, gup_ref,
         kk_ref, ka_ref, lng_ref, lnb_ref, rk_ref, y_ref, vout_ref, s_ref) = refs

    @pl.when(pl.program_id(1) == 0)
    def _():
        s_ref[...] = jnp.zeros_like(s_ref)

    keep = pl.program_id(1) != 0
    z = z_ref[...]
    za_all = z + mu_ref[...] * (_shifted(_with_halo(z, zh_ref[...], keep), 1) - z)
    zl = l_ref[...]
    zl_all = zl + mul_ref[...] * (_shifted(_with_halo(zl, lh_ref[...], keep), 1) - zl)
    seg = _seg_ones(HALF_W, 64)

    pairs = range(HALF_W // LANES)
    sls = [slice(p * LANES, (p + 1) * LANES) for p in pairs]
    tri = jnp.where(_chunk_tri(ROWS), 1.0, 0.0).astype(BF16)
    rr = _iota((ROWS, 2 * ROWS), 0)
    cc = _iota((ROWS, 2 * ROWS), 1)
    same_chunk = (rr // CHUNK) == (cc // ROWS)
    rc_strict = same_chunk & ((cc % CHUNK) < (rr % CHUNK))
    rc_incl = same_chunk & ((cc % CHUNK) <= (rr % CHUNK))

    head_blk = (_iota((LANES, LANES), 0) // CHUNK) == (_iota((LANES, LANES), 1) // CHUNK)
    eye_c = jnp.where((_iota((CHUNK, LANES), 1) % CHUNK) == _iota((CHUNK, LANES), 0), 1.0, 0.0)

    def to_bd_c(m_c):
        return jnp.where(head_blk, jnp.concatenate([m_c, m_c], axis=0), 0.0).astype(BF16)
    q, v_s, a_in_s, r_in, bk_t, e_col, post_in = {}, {}, {}, {}, {}, {}, {}
    a_rb, av, tt, r_eff, y_off, mb = {}, {}, {}, {}, {}, {}
    zero_blk = jnp.zeros((LANES, LANES), BF16)

    def prepare(blk):
        rows = slice(blk * ROWS, (blk + 1) * ROWS)
        za = za_all[rows]
        zl = zl_all[rows]
        r = za[:, 0:HALF_W]
        k_raw = za[:, HALF_W:2 * HALF_W]
        v = za[:, 2 * HALF_W:3 * HALF_W]
        if has_vres:
            gate = _sigmoid(v0_ref[...] + _dot(zl, vup_ref[...]))
            v = v + (vf_ref[rows, :] - v) * gate
        else:
            vout_ref[rows, :] = v
        yield
        w_log = -_softplus(-(w0_ref[...] + _dot(jnp.tanh(zl), wup_ref[...]))) - 0.5
        lw = -jnp.exp(w_log)
        a = _sigmoid(a0_ref[...] + _dot(zl, aup_ref[...]))
        g = _dot(_sigmoid(zl), gup_ref[...])
        yield
        kkk = k_raw * kk_ref[...]
        kk = kkk * lax.rsqrt(_dot(kkk * kkk, seg) + L2_EPS)
        k = k_raw * (1.0 + (a - 1.0) * ka_ref[...])
        bonus = _dot_exact_rhs(r * k * rk_ref[...], seg)
        post_in[blk] = (bonus * v, g)
        yield
        b = _dot_exact_lhs(tri, lw)
        bref = _chunk_rows(b, CHUNK // 2, CHUNK + CHUNK // 2)
        blast = _chunk_rows(b, CHUNK - 1, 2 * CHUNK - 1)
        e_pos = jnp.exp(b - bref)
        e_neg = jnp.exp(bref - b)
        e_in = jnp.exp(b)
        e_st = jnp.exp(blast - b)
        e_last = jnp.exp(blast)
        yield
        kka = kk * a
        neg_kk_exc = -(kk * jnp.exp(-lw))
        a_t = neg_kk_exc * e_pos
        a_in = neg_kk_exc * e_in
        r_t = r * e_pos
        r_in_blk = r * e_in
        b_t = kka * e_neg
        b_st = kka * e_st
        k_t = k * e_neg
        k_st = k * e_st
        for p in pairs:
            sl = sls[p]
            un = (blk, p)
            q[un] = _dot_nt(jnp.concatenate([a_t[:, sl], r_t[:, sl]], axis=0),
                            jnp.concatenate([_stack_heads(b_t[:, sl]), _stack_heads(k_t[:, sl])], axis=0))
            v_s[un] = _stack_heads(v[:, sl]).astype(BF16)
            a_in_s[un] = _stack_heads(a_in[:, sl])
            r_in[un] = r_in_blk[:, sl]
            b_s = _stack_heads(b_st[:, sl])
            k_s = _stack_heads(k_st[:, sl])
            for c in range(2):
                ss = slice(c * LANES, (c + 1) * LANES)
                bk_t[un + (c,)] = jnp.concatenate([b_s[ss].T, k_s[ss].T], axis=1).astype(BF16)
                e_col[un + (c,)] = jnp.broadcast_to(e_last[c * CHUNK:c * CHUNK + 1, sl], (LANES, LANES)).T
            yield

    def solve(units):
        a_kk = {}
        for u in units:
            a_rb[u] = jnp.where(rc_incl, q[u][ROWS:2 * ROWS, 0:256], 0.0).astype(BF16)
            a_kk[u] = jnp.concatenate([jnp.where(rc_strict, q[u][0:ROWS, 256:512], 0.0),
                                       jnp.where(rc_incl, q[u][ROWS:2 * ROWS, 256:512], 0.0)], axis=0)
        yield
        n_list = [jnp.where(rc_strict, q[u][0:ROWS, 0:256], 0.0)[c * CHUNK:(c + 1) * CHUNK, c * LANES:(c + 1) * LANES]
                  for u in units for c in range(2)]
        t_list = []
        yield from _neumann_steps(n_list, eye_c, to_bd_c, t_list)
        for u in units:
            av[u] = _dot(a_kk[u], v_s[u])
        yield
        for i, u in enumerate(units):
            x_s = jnp.concatenate([a_in_s[u], _stack_heads(av[u][0:ROWS])], axis=1).astype(BF16)
            tt[u] = jnp.concatenate([_dot(t_list[2 * i + c], x_s[c * LANES:(c + 1) * LANES]) for c in range(2)],
                                    axis=0)
        yield
        wu_s = {u: jnp.concatenate([_stack_heads(tt[u][:, 0:LANES]), _stack_heads(tt[u][:, LANES:2 * LANES])],
                                   axis=1).astype(BF16) for u in units}
        for u in units:
            ru = _dot(a_rb[u], wu_s[u])
            r_eff[u] = (r_in[u] + ru[:, 0:LANES]).astype(BF16)
            y_off[u] = ru[:, LANES:2 * LANES] + av[u][ROWS:2 * ROWS]
        yield
        for c in range(2):
            ss = slice(c * LANES, (c + 1) * LANES)
            for u in units:
                rhs = jnp.concatenate([wu_s[u][ss], jnp.concatenate([zero_blk, v_s[u][ss]], axis=1)], axis=0)
                mb[u + (c,)] = _dot(bk_t[u + (c,)], rhs)
            yield

    state = [s_ref[p] for p in pairs]

    def sequential(blk):
        rows = slice(blk * ROWS, (blk + 1) * ROWS)
        for c in range(2):
            rs = slice(c * CHUNK, (c + 1) * CHUNK)
            out_rows = slice(blk * ROWS + c * CHUNK, blk * ROWS + (c + 1) * CHUNK)
            for p in pairs:
                un = (blk, p)
                h_b = state[p].astype(BF16)
                y_ref[out_rows, sls[p]] = _dot(r_eff[un][rs], h_b) + y_off[un][rs]
                m_b = mb[un + (c,)]
                state[p] = state[p] * e_col[un + (c,)] + _dot(m_b[:, 0:LANES], h_b) + m_b[:, LANES:2 * LANES]
            yield
        y = y_ref[rows, :]
        mu_y = _dot_exact_rhs(y, seg) * (1.0 / 64)
        d = y - mu_y
        var = _dot(d * d, seg) * (1.0 / 64)
        bonus_v, g = post_in[blk]
        y_ref[rows, :] = (d * lax.rsqrt(var + RWKV_LN_EPS) * lng_ref[...] + lnb_ref[...] + bonus_v) * g
        yield

    first, second = list(range(nb // 2)), list(range(nb // 2, nb))

    def each(fn, blks):
        for blk in blks:
            yield from fn(blk)

    _round_robin(each(prepare, first))
    _round_robin(solve([(blk, p) for blk in first for p in pairs]), each(prepare, second))
    _round_robin(solve([(blk, p) for blk in second for p in pairs]), each(sequential, first))
    _round_robin(each(sequential, second))
    for p in pairs:
        s_ref[p] = state[p]


def _rwkv(z_rkv, z_lr, v_first, prm, ln_g, ln_b, r_k, batch, seq_len, nb=4):
    n = z_rkv.shape[0]
    has_vres = v_first is not None
    rows = nb * ROWS
    nblk = seq_len // rows
    w3 = 3 * HALF_W

    def tile(width):
        return pl.BlockSpec((rows, width), lambda b, j: (b * nblk + j, 0))

    def halo(width):
        step = rows // HALO
        return pl.BlockSpec((HALO, width), lambda b, j: (jnp.maximum((b * nblk + j) * step - 1, 0), 0))

    ins = [z_rkv, z_rkv, z_lr, z_lr]
    specs = [tile(w3), halo(w3), tile(A_LR_W), halo(A_LR_W)]
    if has_vres:
        ins.append(v_first)
        specs.append(tile(HALF_W))
    names = ["mu", "mul", "w0", "wup", "a0", "aup", "gup", "k_k", "k_a"] + (["v0", "vup"] if has_vres else [])
    for nm in names:
        ins.append(prm[nm])
        specs.append(_const_spec(prm[nm].shape))
    ins += [ln_g, ln_b, r_k]
    specs += [_const_spec((1, HALF_W))] * 3
    n_out = 1 if has_vres else 2
    out = pl.pallas_call(
        functools.partial(_rwkv_body, has_vres=has_vres, nb=nb),
        grid=(batch, nblk),
        in_specs=specs,
        out_specs=[tile(HALF_W)] * n_out,
        out_shape=[jax.ShapeDtypeStruct((n, HALF_W), F32)] * n_out,
        scratch_shapes=[pltpu.VMEM((HALF_W // LANES, LANES, LANES), F32)],
        compiler_params=_params(("parallel", "arbitrary")),
        name="rwkv",
    )(*ins)
    return (out[0], None) if has_vres else (out[0], out[1])


def _gelu_erf(x):
    return 0.5 * x * (1.0 + lax.erf(x * (2.0 ** -0.5)))


def _sgu_body(u_ref, v_ref, lng_ref, lnb_ref, w_ref, bias_ref, o_ref, *, tm):
    seg = _seg_ones(HALF_W, 64)
    u = _gelu_erf(u_ref[...])
    v = _gelu_erf(v_ref[...])
    mu = _dot_exact_rhs(v, seg) * (1.0 / 64)
    d = v - mu
    var = _dot(d * d, seg) * (1.0 / 64)
    vn = d * lax.rsqrt(var + SGU_LN_EPS) * lng_ref[...] + lnb_ref[...]
    r = _iota((SGU_BLOCK, 2 * SGU_BLOCK), 0)
    c = _iota((SGU_BLOCK, 2 * SGU_BLOCK), 1)
    causal = (r // CHUNK) >= ((c % SGU_BLOCK) // CHUNK)
    lane = _iota((SGU_BLOCK, LANES), 1)
    m1 = lane < 64
    bias = bias_ref[...]
    for p in range(HALF_W // LANES):
        sl = slice(p * LANES, (p + 1) * LANES)
        w = jnp.where(causal, w_ref[p], 0.0)
        for nb in range(tm // SGU_BLOCK):
            rs = slice(nb * SGU_BLOCK, (nb + 1) * SGU_BLOCK)
            blk = vn[rs, sl]
            stacked = jnp.concatenate([jnp.where(m1, blk, 0.0), jnp.where(m1, 0.0, blk)], axis=0)
            mixed = _dot(w, stacked) + bias[:, sl]
            o_ref[rs, sl] = u[rs, sl] * mixed


def _sgu(z_b, ln_g, ln_b, w_pairs, bias, tm=512):
    n = z_b.shape[0]
    return pl.pallas_call(
        functools.partial(_sgu_body, tm=tm),
        grid=(n // tm,),
        in_specs=[_row_spec(tm, HALF_W, 0), _row_spec(tm, HALF_W, 1),
                  _const_spec((1, HALF_W)), _const_spec((1, HALF_W)),
                  _const_spec(w_pairs.shape), _const_spec(bias.shape)],
        out_specs=_row_spec(tm, HALF_W),
        out_shape=jax.ShapeDtypeStruct((n, HALF_W), F32),
        compiler_params=_params(("parallel",)),
        name="sgu",
    )(z_b, z_b, ln_g, ln_b, w_pairs, bias)


def _hgrn_body(q_ref, f_ref, i_ref, gt_ref, lbl_ref, ng_ref, y_ref, s_ref, *, layer, nb):
    @pl.when(pl.program_id(1) == 0)
    def _():
        s_ref[...] = jnp.zeros_like(s_ref)

    logits = lbl_ref[...]
    e = jnp.exp(logits - jnp.max(logits, axis=0, keepdims=True))
    prob = e / jnp.sum(e, axis=0, keepdims=True)
    lb = jnp.sum(prob[0:layer + 1], axis=0, keepdims=True) - prob[0:1]

    f = lb + (1.0 - lb) * _sigmoid(f_ref[...])
    lf_all = jnp.log(f)
    kx_all = 1.0 - f
    q = q_ref[...]
    qs_all = q * _sigmoid(q)
    tri = jnp.where(_chunk_tri(ROWS), 1.0, 0.0).astype(BF16)
    rr = _iota((2 * ROWS, ROWS), 0)
    cc = _iota((2 * ROWS, ROWS), 1)
    valid = ((rr // ROWS) == (cc // CHUNK)) & ((cc % CHUNK) <= (rr % CHUNK))
    m1 = _iota((CHUNK, LANES), 1) < 64
    blockmask = (_iota((LANES, LANES), 0) // 64) == (_iota((LANES, LANES), 1) // 64)
    pairs = range(HALF_W // LANES)
    state = [s_ref[p] for p in pairs]

    for blk in range(nb):
        rows = slice(blk * ROWS, (blk + 1) * ROWS)
        qs = qs_all[rows]
        kx = kx_all[rows]
        v = i_ref[rows, :]
        b = _dot_exact_lhs(tri, lf_all[rows])
        bref = _chunk_rows(b, CHUNK // 2, CHUNK + CHUNK // 2)
        blast = _chunk_rows(b, CHUNK - 1, 2 * CHUNK - 1)
        q_t = qs * jnp.exp(b - bref)
        k_t = kx * jnp.exp(bref - b)
        q_in = qs * jnp.exp(b)
        k_st = kx * jnp.exp(blast - b)
        e_last = jnp.exp(blast)
        for p in pairs:
            sl = slice(p * LANES, (p + 1) * LANES)
            sc = _dot_nt(_stack_heads(q_t[:, sl]), k_t[:, sl])
            oi = _dot(jnp.where(valid, sc, 0.0), v[:, sl])
            for c in range(2):
                rs = slice(c * CHUNK, (c + 1) * CHUNK)
                o_inter = _dot_nt(q_in[rs, sl], state[p])
                o_intra = (jnp.where(m1, oi[c * 128:c * 128 + 64], 0.0)
                           + jnp.where(m1, 0.0, oi[c * 128 + 64:c * 128 + 128]))
                y_ref[blk * ROWS + c * CHUNK:blk * ROWS + (c + 1) * CHUNK, sl] = o_inter + o_intra
                upd = jnp.where(blockmask, _dot_tn(v[rs, sl], k_st[rs, sl]), 0.0)
                state[p] = state[p] * e_last[c * CHUNK:c * CHUNK + 1, sl] + upd
    for p in pairs:
        s_ref[p] = state[p]

    o = y_ref[...]
    ms = _dot(o * o, _seg_ones(HALF_W, 64)) * (1.0 / 64)
    gate = gt_ref[...]
    y_ref[...] = o * lax.rsqrt(ms + NORM_EPS) * ng_ref[...] * (gate * _sigmoid(gate))


def _hgrn(z_c, lb_logits, norm_g, layer, batch, seq_len, nb=4):
    n = z_c.shape[0]
    rows = nb * ROWS
    nblk = seq_len // rows

    def spec(col):
        return pl.BlockSpec((rows, HALF_W), lambda b, j: (b * nblk + j, col))

    return pl.pallas_call(
        functools.partial(_hgrn_body, layer=layer, nb=nb),
        grid=(batch, nblk),
        in_specs=[spec(c) for c in range(4)] + [_const_spec(lb_logits.shape), _const_spec((1, HALF_W))],
        out_specs=spec(0),
        out_shape=jax.ShapeDtypeStruct((n, HALF_W), F32),
        scratch_shapes=[pltpu.VMEM((HALF_W // LANES, LANES, LANES), F32)],
        compiler_params=_params(("parallel", "arbitrary")),
        name="hgrn",
    )(z_c, z_c, z_c, z_c, lb_logits, norm_g)


def _gdn_body(z_ref, zh_ref, ba_ref, zz_ref, cw_ref, alog_ref, dtb_ref, eb_ref, ea_ref, ng_ref, y_ref, s_ref, *, nb):
    @pl.when(pl.program_id(1) == 0)
    def _():
        s_ref[...] = jnp.zeros_like(s_ref)

    keep = pl.program_id(1) != 0
    ext = _with_halo(z_ref[...], zh_ref[...], keep)
    cw = cw_ref[...]
    seg = _seg_ones(HALF_W, D_HD)
    tri = jnp.where(_chunk_tri(ROWS), 1.0, 0.0).astype(BF16)
    incl = _chunk_tri(ROWS)
    strict = _chunk_tri(ROWS, strict=True)
    heads = range(D_HEADS)
    sls = [slice(h * D_HD, (h + 1) * D_HD) for h in heads]
    eye = jnp.where(_iota((ROWS, ROWS), 0) == _iota((ROWS, ROWS), 1), 1.0, 0.0)
    eye2 = jnp.concatenate([eye, eye], axis=1)
    zero_blk = jnp.zeros((ROWS, D_HD), BF16)
    chunk_row = _iota((ROWS, 2 * D_HD), 0) // CHUNK

    def to_bd(m_rc):
        mb = m_rc.astype(BF16)
        return jnp.concatenate([jnp.concatenate([mb[:, 0:D_HD], zero_blk], axis=1),
                                jnp.concatenate([zero_blk, mb[:, D_HD:2 * D_HD]], axis=1)], axis=0)

    neg_lower, qk, rhs, wq, k_st_t, d_last = {}, {}, {}, {}, {}, {}
    sol, qo, q_eff, kuw = {}, {}, {}, {}

    def prepare(blk):
        rows = slice(blk * ROWS, (blk + 1) * ROWS)
        win = ext[blk * ROWS:blk * ROWS + ROWS + HALO]
        acc = win[HALO:] * cw[D_CONV - 1:D_CONV]
        for j in range(D_CONV - 1):
            acc = acc + _shifted(win, D_CONV - 1 - j) * cw[j:j + 1]
        yield
        qkv = acc * _sigmoid(acc)
        q_raw = qkv[:, 0:HALF_W]
        k_raw = qkv[:, HALF_W:2 * HALF_W]
        v_all = qkv[:, 2 * HALF_W:3 * HALF_W]
        q_all = q_raw * lax.rsqrt(_dot(q_raw * q_raw, seg) + L2_EPS) * (D_HD ** -0.5)
        k_all = k_raw * lax.rsqrt(_dot(k_raw * k_raw, seg) + L2_EPS)
        yield
        ba = ba_ref[rows, :]
        beta_all = _sigmoid(_dot_exact_rhs(ba, eb_ref[...]))
        g_all = -jnp.exp(alog_ref[...]) * _softplus(_dot_exact_rhs(ba, ea_ref[...]) + dtb_ref[...])
        gc_all = _dot_exact_lhs(tri, g_all)
        glast_all = _chunk_rows(gc_all, CHUNK - 1, 2 * CHUNK - 1)
        yield
        for h in heads:
            un = (blk, h)
            gc = gc_all[:, sls[h]]
            glast = glast_all[:, sls[h]]
            diff = gc - gc.T
            decay = jnp.where(incl, jnp.exp(jnp.where(incl, diff, 0.0)), 0.0)
            q = q_all[:, sls[h]]
            k = k_all[:, sls[h]]
            beta = beta_all[:, sls[h]]
            kb = k * beta
            kq = _dot_nt(jnp.concatenate([kb, q], axis=0), k)
            neg_lower[un] = -jnp.where(strict, kq[0:ROWS] * decay, 0.0)
            qk[un] = (kq[ROWS:2 * ROWS] * decay).astype(BF16)
            egc = jnp.exp(gc)
            rhs[un] = jnp.concatenate([v_all[:, sls[h]] * beta, kb * egc], axis=1)
            wq[un] = q * egc
            k_st_t[un] = (k * jnp.exp(glast - gc)).T.astype(BF16)
            d_last[un] = jnp.exp(glast)
            yield

    def solve(blks):
        units = [(blk, h) for blk in blks for h in heads]
        pair_units = [(blk, h) for blk in blks for h in range(0, D_HEADS, 2)]
        t_list = []
        yield from _neumann_steps([jnp.concatenate([neg_lower[(blk, h)], neg_lower[(blk, h + 1)]], axis=1)
                                   for blk, h in pair_units], eye2, to_bd, t_list)
        t_pairs = dict(zip(pair_units, t_list))
        for blk, h in units:
            sol[(blk, h)] = _dot(t_pairs[(blk, h - h % 2)][:, (h % 2) * D_HD:(h % 2 + 1) * D_HD], rhs[(blk, h)])
        yield
        for un in units:
            qo[un] = _dot(qk[un], sol[un])
            q_eff[un] = (wq[un] - qo[un][:, D_HD:2 * D_HD]).astype(BF16)
        yield
        for c in range(2):
            for un in units:
                kuw[un + (c,)] = _dot(k_st_t[un], jnp.where(chunk_row == c, sol[un], 0.0))
            yield

    state = [s_ref[h] for h in heads]

    def sequential(blk):
        for c in range(2):
            rs = slice(c * CHUNK, (c + 1) * CHUNK)
            out_rows = slice(blk * ROWS + c * CHUNK, blk * ROWS + (c + 1) * CHUNK)
            for h in heads:
                un = (blk, h)
                s_b = state[h].astype(BF16)
                o = _dot(q_eff[un][rs], s_b) + qo[un][rs, 0:D_HD]
                k_uw = kuw[un + (c,)]
                state[h] = (state[h] * d_last[un][c * CHUNK:c * CHUNK + 1] + k_uw[:, 0:D_HD]
                            - _dot(k_uw[:, D_HD:2 * D_HD], s_b))
                z = zz_ref[out_rows, sls[h]]
                y_ref[out_rows, sls[h]] = _rms(o, ng_ref[...]) * (z * _sigmoid(z))
            yield

    first, second = list(range(nb // 2)), list(range(nb // 2, nb))

    def each(fn, blks):
        for blk in blks:
            yield from fn(blk)

    _round_robin(each(prepare, first))
    _round_robin(solve(first), each(prepare, second))
    _round_robin(solve(second), each(sequential, first))
    _round_robin(each(sequential, second))
    for h in heads:
        s_ref[h] = state[h]


def _gdn(z_qkv, z_ba, z_z, conv_w, a_log_full, dt_bias_full, e_b, e_a, norm_g, batch, seq_len, nb=4):
    n = z_qkv.shape[0]
    rows = nb * ROWS
    nblk = seq_len // rows
    w3 = 3 * HALF_W

    def tile(width):
        return pl.BlockSpec((rows, width), lambda b, j: (b * nblk + j, 0))

    step = rows // HALO
    halo = pl.BlockSpec((HALO, w3), lambda b, j: (jnp.maximum((b * nblk + j) * step - 1, 0), 0))
    return pl.pallas_call(
        functools.partial(_gdn_body, nb=nb),
        grid=(batch, nblk),
        in_specs=[tile(w3), halo, tile(LANES), tile(HALF_W),
                  _const_spec((D_CONV, w3)), _const_spec((1, HALF_W)), _const_spec((1, HALF_W)),
                  _const_spec((LANES, HALF_W)), _const_spec((LANES, HALF_W)), _const_spec((1, D_HD))],
        out_specs=tile(HALF_W),
        out_shape=jax.ShapeDtypeStruct((n, HALF_W), F32),
        scratch_shapes=[pltpu.VMEM((D_HEADS, D_HD, D_HD), F32)],
        compiler_params=_params(("parallel", "arbitrary")),
        name="gdn",
    )(z_qkv, z_qkv, z_ba, z_z, conv_w, a_log_full, dt_bias_full, e_b, e_a, norm_g)


def _gelu_tanh(x):
    return 0.5 * x * (1.0 + jnp.tanh((2.0 / jnp.pi) ** 0.5 * (x + 0.044715 * (x * x * x))))


def _mix_ffn_body(x_ref, xh_ref, ya_ref, yah_ref, yb_ref, ybh_ref, wo_ref, gmix_ref, gpre_ref,
                  win_ref, cw_ref, cb_ref, wout_ref, gpost_ref, o_ref, *, tiles_per_batch):
    keep = (pl.program_id(0) % tiles_per_batch) != 0
    xe = jnp.concatenate([xh_ref[...], x_ref[...]], axis=0)
    yae = jnp.concatenate([yah_ref[...], ya_ref[...]], axis=0)
    ybe = jnp.concatenate([ybh_ref[...], yb_ref[...]], axis=0)
    mix = _dot(yae, wo_ref[0:HALF_W, :]) + _dot(ybe, wo_ref[HALF_W:2 * HALF_W, :])
    x1 = xe + _rms(mix, gmix_ref[...])
    h = _rms(x1, gpre_ref[...]).astype(BF16)
    cw = cw_ref[...]
    cb = cb_ref[...]
    ffw = D_FF // FF_SPLIT
    y = None
    for j in range(FF_SPLIT):
        cols = slice(j * ffw, (j + 1) * ffw)
        gate = _dot(h, win_ref[:, cols])
        ext = jnp.concatenate([jnp.where(keep, gate[0:HALO], 0.0), gate[HALO:]], axis=0)
        up = _dot(h[HALO:], win_ref[:, D_FF + j * ffw:D_FF + (j + 1) * ffw])
        conv = (ext[HALO:] * cw[2:3, cols] + _shifted(ext, 1) * cw[1:2, cols]
                + _shifted(ext, 2) * cw[0:1, cols] + cb[:, cols])
        part = _dot(_gelu_tanh(conv) * up, wout_ref[cols, :])
        y = part if y is None else y + part
    o_ref[...] = x1[HALO:] + _rms(y, gpost_ref[...])


def _mix_ffn(x, ya, yb, w_o_all, mix_layer, g_mix, g_pre, w_in_all, conv_w, conv_b, w_out_all, g_post, layer,
             seq_len, tm=512):
    n, d = x.shape
    return pl.pallas_call(
        functools.partial(_mix_ffn_body, tiles_per_batch=seq_len // tm),
        grid=(n // tm,),
        in_specs=[_row_spec(tm, d), _halo_spec(tm, d),
                  _row_spec(tm, HALF_W), _halo_spec(tm, HALF_W), _row_spec(tm, HALF_W), _halo_spec(tm, HALF_W),
                  _layer_spec(w_o_all.shape, mix_layer), _const_spec((1, d)), _const_spec((1, d)),
                  _layer_spec(w_in_all.shape, layer), _const_spec((3, D_FF)), _const_spec((1, D_FF)),
                  _layer_spec(w_out_all.shape, layer), _const_spec((1, d))],
        out_specs=_row_spec(tm, d),
        out_shape=jax.ShapeDtypeStruct((n, d), F32),
        compiler_params=_params(("parallel",)),
        name="mix_ffn",
    )(x, x, ya, ya, yb, yb, w_o_all, g_mix, g_pre, w_in_all, conv_w, conv_b, w_out_all, g_post)


def _pad_rows(w, start, total):
    return jnp.zeros((total, w.shape[1]), w.dtype).at[start:start + w.shape[0]].set(w)


def _even_in_weights(ev_w_in, rwkv_vres_down):
    a_cols = 3 * HALF_W + A_DECAY_R + A_AAA_R + A_GATE_R
    n_even, d, _ = ev_w_in.shape
    w = ev_w_in.astype(BF16)
    vres = jnp.concatenate([jnp.zeros((1, d, A_MV_R), BF16), rwkv_vres_down.astype(BF16)], axis=0)
    pad = jnp.zeros((n_even, d, A_LR_W - (a_cols - 3 * HALF_W) - A_MV_R), BF16)
    return jnp.concatenate([w[:, :, 0:a_cols], vres, pad, w[:, :, a_cols:]], axis=2)


def _even_params(e, rwkv_mu, rwkv_w0, rwkv_w_up, rwkv_a0, rwkv_a_up, rwkv_g_up, rwkv_k_k, rwkv_k_a,
                 rwkv_vres_up, rwkv_v0):
    lr_w = A_DECAY_R + A_AAA_R + A_GATE_R
    mu = rwkv_mu[e]
    o_w, o_a, o_g, o_v = 0, A_DECAY_R, A_DECAY_R + A_AAA_R, lr_w
    prm = {
        "mu": mu[None, 0:3 * HALF_W],
        "mul": jnp.concatenate([mu[3 * HALF_W:], jnp.zeros((A_LR_W - lr_w,), F32)])[None],
        "w0": rwkv_w0[e][None],
        "wup": _pad_rows(rwkv_w_up[e], o_w, A_LR_W),
        "a0": rwkv_a0[e][None],
        "aup": _pad_rows(rwkv_a_up[e], o_a, A_LR_W),
        "gup": _pad_rows(rwkv_g_up[e], o_g, A_LR_W),
        "k_k": rwkv_k_k[e][None],
        "k_a": rwkv_k_a[e][None],
    }
    if e > 0:
        prm["v0"] = rwkv_v0[e - 1][None]
        prm["vup"] = _pad_rows(rwkv_vres_up[e - 1], o_v, A_LR_W)
    return prm


def _odd_in_weights(od_w_in):
    return jnp.pad(od_w_in.astype(BF16), ((0, 0), (0, 0), (0, LANES - 2 * D_HEADS)))


def _head_expand(offset):
    r = jnp.arange(LANES)[:, None]
    c = jnp.arange(HALF_W)[None, :]
    return (r == offset + c // D_HD).astype(BF16)


def kernel(x, norm_mix_pre, norm_mix_post, norm_ffn_pre, norm_ffn_post, ev_w_in, ev_w_out, rwkv_mu, rwkv_w0, rwkv_w_up, rwkv_a0, rwkv_a_up, rwkv_g_up, rwkv_k_k, rwkv_k_a, rwkv_r_k, rwkv_ln_g, rwkv_ln_b, rwkv_vres_down, rwkv_vres_up, rwkv_v0, sgu_ln_g, sgu_ln_b, sgu_w, sgu_b, od_w_in, od_w_out, hgrn_lb_logits, hgrn_norm_g, gdn_conv_w, gdn_a_log, gdn_dt_bias, gdn_norm_g, ffn_w_in, ffn_conv_w, ffn_conv_b, ffn_w_out):
    batch, seq_len, d = x.shape
    depth = norm_mix_pre.shape[0]
    n = batch * seq_len
    xf = x.reshape(n, d)
    ev_w_in_p = _even_in_weights(ev_w_in, rwkv_vres_down)
    od_w_in_p = _odd_in_weights(od_w_in)
    ev_w_out_b = ev_w_out.astype(BF16)
    od_w_out_b = od_w_out.astype(BF16)
    ffn_w_in_b = ffn_w_in.astype(BF16)
    ffn_w_out_b = ffn_w_out.astype(BF16)
    v_first = None
    for l in range(depth):
        if l % 2 == 0:
            e = l // 2
            prm = _even_params(e, rwkv_mu, rwkv_w0, rwkv_w_up, rwkv_a0, rwkv_a_up, rwkv_g_up, rwkv_k_k, rwkv_k_a,
                               rwkv_vres_up, rwkv_v0)
            z_rkv, z_lr, z_b = _inproj(xf, norm_mix_pre[l][None], ev_w_in_p, e, (3 * HALF_W, A_LR_W, 2 * HALF_W))
            ya, v_a = _rwkv(z_rkv, z_lr, v_first if e > 0 else None, prm, rwkv_ln_g[e][None], rwkv_ln_b[e][None],
                            rwkv_r_k[e].reshape(1, HALF_W), batch, seq_len)
            if e == 0:
                v_first = v_a
            w_s = sgu_w[e]
            w_pairs = jnp.concatenate([w_s[0::2], w_s[1::2]], axis=2)
            bias = jnp.repeat(sgu_b[e].T, HALF_W // B_GROUPS, axis=1)
            yb = _sgu(z_b, sgu_ln_g[e][None], sgu_ln_b[e][None], w_pairs, bias)
            w_out_all, mix_layer = ev_w_out_b, e
        else:
            o = l // 2
            z_c, z_qkv, z_z, z_ba = _inproj(xf, norm_mix_pre[l][None], od_w_in_p, o,
                                            (4 * HALF_W, 3 * HALF_W, HALF_W, LANES))
            ya = _hgrn(z_c, hgrn_lb_logits, hgrn_norm_g[o][None], o, batch, seq_len)
            yb = _gdn(z_qkv, z_ba, z_z, gdn_conv_w[o], jnp.repeat(gdn_a_log[o], D_HD)[None],
                      jnp.repeat(gdn_dt_bias[o], D_HD)[None], _head_expand(0), _head_expand(D_HEADS),
                      gdn_norm_g[o][None], batch, seq_len)
            w_out_all, mix_layer = od_w_out_b, o
        xf = _mix_ffn(xf, ya, yb, w_out_all, mix_layer, norm_mix_post[l][None], norm_ffn_pre[l][None],
                      ffn_w_in_b, ffn_conv_w[l], ffn_conv_b[l][None], ffn_w_out_b, norm_ffn_post[l][None], l,
                      seq_len)
    return xf.reshape(batch, seq_len, d)
```

```python
import functools

import jax
import jax.numpy as jnp
from jax import lax
from jax.experimental import pallas as pl
from jax.experimental.pallas import tpu as pltpu

F32 = jnp.float32
BF16 = jnp.bfloat16

D_MODEL = 1024
CHUNK = 64
HALF_W = 512
A_DECAY_R = 32
A_AAA_R = 32
A_MV_R = 32
A_GATE_R = 96
A_LR_W = 256
SGU_BLOCK = 128
B_GROUPS = 8
D_HEADS = 4
D_HD = 128
D_CONV = 4
D_FF = 2816
NORM_EPS = 1e-6
RWKV_LN_EPS = 64e-5
SGU_LN_EPS = 1e-5
L2_EPS = 1e-6

ROWS = 128
HALO = 16
FF_SPLIT = 1
LANES = 128
VMEM_LIMIT = 56 * 1024 * 1024


def _dot(a, b):
    return jnp.dot(a.astype(BF16), b.astype(BF16), preferred_element_type=F32)


def _dot_nt(a, b):
    return lax.dot_general(a.astype(BF16), b.astype(BF16), (((1,), (1,)), ((), ())), preferred_element_type=F32)


def _dot_tn(a, b):
    return lax.dot_general(a.astype(BF16), b.astype(BF16), (((0,), (0,)), ((), ())), preferred_element_type=F32)


def _dot_exact_lhs(m_bf16, x):
    hi = x.astype(BF16)
    lo = (x - hi.astype(F32)).astype(BF16)
    return _dot(m_bf16, hi) + _dot(m_bf16, lo)


def _dot_exact_rhs(x, m_bf16):
    hi = x.astype(BF16)
    lo = (x - hi.astype(F32)).astype(BF16)
    return _dot(hi, m_bf16) + _dot(lo, m_bf16)


def _sigmoid(x):
    return 1.0 / (1.0 + jnp.exp(-x))


def _softplus(x):
    return jnp.maximum(x, 0.0) + jnp.log(1.0 + jnp.exp(-jnp.abs(x)))


def _rms(x, g):
    return x * lax.rsqrt(jnp.mean(x * x, axis=-1, keepdims=True) + NORM_EPS) * g


def _iota(shape, dim):
    return lax.broadcasted_iota(jnp.int32, shape, dim)


def _chunk_tri(n, strict=False):
    r = _iota((n, n), 0)
    c = _iota((n, n), 1)
    same = (r // CHUNK) == (c // CHUNK)
    return same & ((c < r) if strict else (c <= r))


def _seg_ones(width, seg):
    r = _iota((width, width), 0)
    c = _iota((width, width), 1)
    return jnp.where((r // seg) == (c // seg), 1.0, 0.0).astype(BF16)


def _stack_heads(x):
    lane = _iota((CHUNK, LANES), 1)
    m1 = lane < 64
    top, bot = x[0:CHUNK], x[CHUNK:2 * CHUNK]
    return jnp.concatenate([jnp.where(m1, top, 0.0), jnp.where(m1, 0.0, top),
                            jnp.where(m1, bot, 0.0), jnp.where(m1, 0.0, bot)], axis=0)


def _neumann_steps(n_mats, eye, to_bd, out):
    ps = [eye + n for n in n_mats]
    ms = list(n_mats)
    bds = [to_bd(m) for m in ms]
    for _ in range(5):
        ms = [_dot(m, bd) for m, bd in zip(ms, bds)]
        yield
        bds = [to_bd(m) for m in ms]
        ps = [p + _dot(p, bd) for p, bd in zip(ps, bds)]
        yield
    out.extend(ps)


def _round_robin(*gens):
    gens = list(gens)
    while gens:
        for g in list(gens):
            try:
                next(g)
            except StopIteration:
                gens.remove(g)


def _chunk_rows(x, r0, r1):
    w = x.shape[1]
    return jnp.concatenate([jnp.broadcast_to(x[r0:r0 + 1], (CHUNK, w)),
                            jnp.broadcast_to(x[r1:r1 + 1], (CHUNK, w))], axis=0)


def _with_halo(x, halo, keep):
    return jnp.concatenate([jnp.where(keep, halo, 0.0), x], axis=0)


def _shifted(ext, s):
    if s == 0:
        return ext[HALO:]
    return pltpu.roll(ext, s, 0)[HALO:]


def _params(sem):
    return pltpu.CompilerParams(dimension_semantics=sem, vmem_limit_bytes=VMEM_LIMIT)


def _layer_spec(shape, layer):
    return pl.BlockSpec((None,) + tuple(shape[1:]), lambda *_: (layer, 0, 0), pipeline_mode=pl.Buffered(1))


def _row_spec(tm, width, col=0):
    return pl.BlockSpec((tm, width), lambda i, col=col: (i, col))


def _halo_spec(tm, width, col=0):
    step = tm // HALO
    return pl.BlockSpec((HALO, width), lambda i, col=col: (jnp.maximum(i * step - 1, 0), col))


def _const_spec(shape):
    return pl.BlockSpec(shape, lambda *_: (0,) * len(shape), pipeline_mode=pl.Buffered(1))


def _inproj_body(x_ref, g_ref, w_ref, *o_refs, widths):
    h = _rms(x_ref[...], g_ref[...]).astype(BF16)
    off = 0
    for o_ref, wd in zip(o_refs, widths):
        o_ref[...] = _dot(h, w_ref[:, off:off + wd])
        off += wd


def _inproj(x, g, w_all, layer, widths, tm=512):
    n, d = x.shape
    return pl.pallas_call(
        functools.partial(_inproj_body, widths=widths),
        grid=(n // tm,),
        in_specs=[_row_spec(tm, d), _const_spec((1, d)), _layer_spec(w_all.shape, layer)],
        out_specs=[_row_spec(tm, wd) for wd in widths],
        out_shape=[jax.ShapeDtypeStruct((n, wd), F32) for wd in widths],
        compiler_params=_params(("parallel",)),
        name="inproj",
    )(x, g, w_all)


def _rwkv_body(*refs, has_vres, nb):
    if has_vres:
        (z_ref, zh_ref, l_ref, lh_ref, vf_ref, mu_ref, mul_ref, w0_ref, wup_ref, a0_ref, aup_ref, gup_ref,
         kk_ref, ka_ref, v0_ref, vup_ref, lng_ref, lnb_ref, rk_ref, y_ref, s_ref) = refs
    else:
        (z_ref, zh_ref, l_ref, lh_ref, mu_ref, mul_ref, w0_ref, wup_ref, a0_ref, aup_ref, gup_ref,
         kk_ref, ka_ref, lng_ref, lnb_ref, rk_ref, y_ref, vout_ref, s_ref) = refs

    @pl.when(pl.program_id(1) == 0)
    def _():
        s_ref[...] = jnp.zeros_like(s_ref)

    keep = pl.program_id(1) != 0
    z = z_ref[...]
    za_all = z + mu_ref[...] * (_shifted(_with_halo(z, zh_ref[...], keep), 1) - z)
    zl = l_ref[...]
    zl_all = zl + mul_ref[...] * (_shifted(_with_halo(zl, lh_ref[...], keep), 1) - zl)
    seg = _seg_ones(HALF_W, 64)

    pairs = range(HALF_W // LANES)
    sls = [slice(p * LANES, (p + 1) * LANES) for p in pairs]
    tri = jnp.where(_chunk_tri(ROWS), 1.0, 0.0).astype(BF16)
    rr = _iota((ROWS, 2 * ROWS), 0)
    cc = _iota((ROWS, 2 * ROWS), 1)
    same_chunk = (rr // CHUNK) == (cc // ROWS)
    rc_strict = same_chunk & ((cc % CHUNK) < (rr % CHUNK))
    rc_incl = same_chunk & ((cc % CHUNK) <= (rr % CHUNK))

    head_blk = (_iota((LANES, LANES), 0) // CHUNK) == (_iota((LANES, LANES), 1) // CHUNK)
    eye_c = jnp.where((_iota((CHUNK, LANES), 1) % CHUNK) == _iota((CHUNK, LANES), 0), 1.0, 0.0)

    def to_bd_c(m_c):
        return jnp.where(head_blk, jnp.concatenate([m_c, m_c], axis=0), 0.0).astype(BF16)
    q, v_s, a_in_s, r_in, bk_t, e_col, post_in = {}, {}, {}, {}, {}, {}, {}
    a_rb, av, tt, r_eff, y_off, mb = {}, {}, {}, {}, {}, {}
    zero_blk = jnp.zeros((LANES, LANES), BF16)

    def prepare(blk):
        rows = slice(blk * ROWS, (blk + 1) * ROWS)
        za = za_all[rows]
        zl = zl_all[rows]
        r = za[:, 0:HALF_W]
        k_raw = za[:, HALF_W:2 * HALF_W]
        v = za[:, 2 * HALF_W:3 * HALF_W]
        if has_vres:
            gate = _sigmoid(v0_ref[...] + _dot(zl, vup_ref[...]))
            v = v + (vf_ref[rows, :] - v) * gate
        else:
            vout_ref[rows, :] = v
        yield
        w_log = -_softplus(-(w0_ref[...] + _dot(jnp.tanh(zl), wup_ref[...]))) - 0.5
        lw = -jnp.exp(w_log)
        a = _sigmoid(a0_ref[...] + _dot(zl, aup_ref[...]))
        g = _dot(_sigmoid(zl), gup_ref[...])
        yield
        kkk = k_raw * kk_ref[...]
        kk = kkk * lax.rsqrt(_dot(kkk * kkk, seg) + L2_EPS)
        k = k_raw * (1.0 + (a - 1.0) * ka_ref[...])
        bonus = _dot_exact_rhs(r * k * rk_ref[...], seg)
        post_in[blk] = (bonus * v, g)
        yield
        b = _dot_exact_lhs(tri, lw)
        bref = _chunk_rows(b, CHUNK // 2, CHUNK + CHUNK // 2)
        blast = _chunk_rows(b, CHUNK - 1, 2 * CHUNK - 1)
        e_pos = jnp.exp(b - bref)
        e_neg = jnp.exp(bref - b)
        e_in = jnp.exp(b)
        e_st = jnp.exp(blast - b)
        e_last = jnp.exp(blast)
        yield
        kka = kk * a
        neg_kk_exc = -(kk * jnp.exp(-lw))
        a_t = neg_kk_exc * e_pos
        a_in = neg_kk_exc * e_in
        r_t = r * e_pos
        r_in_blk = r * e_in
        b_t = kka * e_neg
        b_st = kka * e_st
        k_t = k * e_neg
        k_st = k * e_st
        for p in pairs:
            sl = sls[p]
            un = (blk, p)
            q[un] = _dot_nt(jnp.concatenate([a_t[:, sl], r_t[:, sl]], axis=0),
                            jnp.concatenate([_stack_heads(b_t[:, sl]), _stack_heads(k_t[:, sl])], axis=0))
            v_s[un] = _stack_heads(v[:, sl]).astype(BF16)
            a_in_s[un] = _stack_heads(a_in[:, sl])
            r_in[un] = r_in_blk[:, sl]
            b_s = _stack_heads(b_st[:, sl])
            k_s = _stack_heads(k_st[:, sl])
            for c in range(2):
                ss = slice(c * LANES, (c + 1) * LANES)
                bk_t[un + (c,)] = jnp.concatenate([b_s[ss].T, k_s[ss].T], axis=1).astype(BF16)
                e_col[un + (c,)] = jnp.broadcast_to(e_last[c * CHUNK:c * CHUNK + 1, sl], (LANES, LANES)).T
            yield

    def solve(units):
        a_kk = {}
        for u in units:
            a_rb[u] = jnp.where(rc_incl, q[u][ROWS:2 * ROWS, 0:256], 0.0).astype(BF16)
            a_kk[u] = jnp.concatenate([jnp.where(rc_strict, q[u][0:ROWS, 256:512], 0.0),
                                       jnp.where(rc_incl, q[u][ROWS:2 * ROWS, 256:512], 0.0)], axis=0)
        yield
        n_list = [jnp.where(rc_strict, q[u][0:ROWS, 0:256], 0.0)[c * CHUNK:(c + 1) * CHUNK, c * LANES:(c + 1) * LANES]
                  for u in units for c in range(2)]
        t_list = []
        yield from _neumann_steps(n_list, eye_c, to_bd_c, t_list)
        for u in units:
            av[u] = _dot(a_kk[u], v_s[u])
        yield
        for i, u in enumerate(units):
            x_s = jnp.concatenate([a_in_s[u], _stack_heads(av[u][0:ROWS])], axis=1).astype(BF16)
            tt[u] = jnp.concatenate([_dot(t_list[2 * i + c], x_s[c * LANES:(c + 1) * LANES]) for c in range(2)],
                                    axis=0)
        yield
        wu_s = {u: jnp.concatenate([_stack_heads(tt[u][:, 0:LANES]), _stack_heads(tt[u][:, LANES:2 * LANES])],
                                   axis=1).astype(BF16) for u in units}
        for u in units:
            ru = _dot(a_rb[u], wu_s[u])
            r_eff[u] = (r_in[u] + ru[:, 0:LANES]).astype(BF16)
            y_off[u] = ru[:, LANES:2 * LANES] + av[u][ROWS:2 * ROWS]
        yield
        for c in range(2):
            ss = slice(c * LANES, (c + 1) * LANES)
            for u in units:
                rhs = jnp.concatenate([wu_s[u][ss], jnp.concatenate([zero_blk, v_s[u][ss]], axis=1)], axis=0)
                mb[u + (c,)] = _dot(bk_t[u + (c,)], rhs)
            yield

    state = [s_ref[p] for p in pairs]

    def sequential(blk):
        rows = slice(blk * ROWS, (blk + 1) * ROWS)
        for c in range(2):
            rs = slice(c * CHUNK, (c + 1) * CHUNK)
            out_rows = slice(blk * ROWS + c * CHUNK, blk * ROWS + (c + 1) * CHUNK)
            for p in pairs:
                un = (blk, p)
                h_b = state[p].astype(BF16)
                y_ref[out_rows, sls[p]] = _dot(r_eff[un][rs], h_b) + y_off[un][rs]
                m_b = mb[un + (c,)]
                state[p] = state[p] * e_col[un + (c,)] + _dot(m_b[:, 0:LANES], h_b) + m_b[:, LANES:2 * LANES]
            yield
        y = y_ref[rows, :]
        mu_y = _dot_exact_rhs(y, seg) * (1.0 / 64)
        d = y - mu_y
        var = _dot(d * d, seg) * (1.0 / 64)
        bonus_v, g = post_in[blk]
        y_ref[rows, :] = (d * lax.rsqrt(var + RWKV_LN_EPS) * lng_ref[...] + lnb_ref[...] + bonus_v) * g
        yield

    first, second = list(range(nb // 2)), list(range(nb // 2, nb))

    def each(fn, blks):
        for blk in blks:
            yield from fn(blk)

    _round_robin(each(prepare, first))
    _round_robin(solve([(blk, p) for blk in first for p in pairs]), each(prepare, second))
    _round_robin(solve([(blk, p) for blk in second for p in pairs]), each(sequential, first))
    _round_robin(each(sequential, second))
    for p in pairs:
        s_ref[p] = state[p]


def _rwkv(z_rkv, z_lr, v_first, prm, ln_g, ln_b, r_k, batch, seq_len, nb=4):
    n = z_rkv.shape[0]
    has_vres = v_first is not None
    rows = nb * ROWS
    nblk = seq_len // rows
    w3 = 3 * HALF_W

    def tile(width):
        return pl.BlockSpec((rows, width), lambda b, j: (b * nblk + j, 0))

    def halo(width):
        step = rows // HALO
        return pl.BlockSpec((HALO, width), lambda b, j: (jnp.maximum((b * nblk + j) * step - 1, 0), 0))

    ins = [z_rkv, z_rkv, z_lr, z_lr]
    specs = [tile(w3), halo(w3), tile(A_LR_W), halo(A_LR_W)]
    if has_vres:
        ins.append(v_first)
        specs.append(tile(HALF_W))
    names = ["mu", "mul", "w0", "wup", "a0", "aup", "gup", "k_k", "k_a"] + (["v0", "vup"] if has_vres else [])
    for nm in names:
        ins.append(prm[nm])
        specs.append(_const_spec(prm[nm].shape))
    ins += [ln_g, ln_b, r_k]
    specs += [_const_spec((1, HALF_W))] * 3
    n_out = 1 if has_vres else 2
    out = pl.pallas_call(
        functools.partial(_rwkv_body, has_vres=has_vres, nb=nb),
        grid=(batch, nblk),
        in_specs=specs,
        out_specs=[tile(HALF_W)] * n_out,
        out_shape=[jax.ShapeDtypeStruct((n, HALF_W), F32)] * n_out,
        scratch_shapes=[pltpu.VMEM((HALF_W // LANES, LANES, LANES), F32)],
        compiler_params=_params(("parallel", "arbitrary")),
        name="rwkv",
    )(*ins)
    return (out[0], None) if has_vres else (out[0], out[1])


def _gelu_erf(x):
    return 0.5 * x * (1.0 + lax.erf(x * (2.0 ** -0.5)))


def _sgu_body(u_ref, v_ref, lng_ref, lnb_ref, w_ref, bias_ref, o_ref, *, tm):
    seg = _seg_ones(HALF_W, 64)
    u = _gelu_erf(u_ref[...])
    v = _gelu_erf(v_ref[...])
    mu = _dot_exact_rhs(v, seg) * (1.0 / 64)
    d = v - mu
    var = _dot(d * d, seg) * (1.0 / 64)
    vn = d * lax.rsqrt(var + SGU_LN_EPS) * lng_ref[...] + lnb_ref[...]
    r = _iota((SGU_BLOCK, 2 * SGU_BLOCK), 0)
    c = _iota((SGU_BLOCK, 2 * SGU_BLOCK), 1)
    causal = (r // CHUNK) >= ((c % SGU_BLOCK) // CHUNK)
    lane = _iota((SGU_BLOCK, LANES), 1)
    m1 = lane < 64
    bias = bias_ref[...]
    for p in range(HALF_W // LANES):
        sl = slice(p * LANES, (p + 1) * LANES)
        w = jnp.where(causal, w_ref[p], 0.0)
        for nb in range(tm // SGU_BLOCK):
            rs = slice(nb * SGU_BLOCK, (nb + 1) * SGU_BLOCK)
            blk = vn[rs, sl]
            stacked = jnp.concatenate([jnp.where(m1, blk, 0.0), jnp.where(m1, 0.0, blk)], axis=0)
            mixed = _dot(w, stacked) + bias[:, sl]
            o_ref[rs, sl] = u[rs, sl] * mixed


def _sgu(z_b, ln_g, ln_b, w_pairs, bias, tm=512):
    n = z_b.shape[0]
    return pl.pallas_call(
        functools.partial(_sgu_body, tm=tm),
        grid=(n // tm,),
        in_specs=[_row_spec(tm, HALF_W, 0), _row_spec(tm, HALF_W, 1),
                  _const_spec((1, HALF_W)), _const_spec((1, HALF_W)),
                  _const_spec(w_pairs.shape), _const_spec(bias.shape)],
        out_specs=_row_spec(tm, HALF_W),
        out_shape=jax.ShapeDtypeStruct((n, HALF_W), F32),
        compiler_params=_params(("parallel",)),
        name="sgu",
    )(z_b, z_b, ln_g, ln_b, w_pairs, bias)


def _hgrn_body(q_ref, f_ref, i_ref, gt_ref, lbl_ref, ng_ref, y_ref, s_ref, *, layer, nb):
    @pl.when(pl.program_id(1) == 0)
    def _():
        s_ref[...] = jnp.zeros_like(s_ref)

    logits = lbl_ref[...]
    e = jnp.exp(logits - jnp.max(logits, axis=0, keepdims=True))
    prob = e / jnp.sum(e, axis=0, keepdims=True)
    lb = jnp.sum(prob[0:layer + 1], axis=0, keepdims=True) - prob[0:1]

    f = lb + (1.0 - lb) * _sigmoid(f_ref[...])
    lf_all = jnp.log(f)
    kx_all = 1.0 - f
    q = q_ref[...]
    qs_all = q * _sigmoid(q)
    tri = jnp.where(_chunk_tri(ROWS), 1.0, 0.0).astype(BF16)
    rr = _iota((2 * ROWS, ROWS), 0)
    cc = _iota((2 * ROWS, ROWS), 1)
    valid = ((rr // ROWS) == (cc // CHUNK)) & ((cc % CHUNK) <= (rr % CHUNK))
    m1 = _iota((CHUNK, LANES), 1) < 64
    blockmask = (_iota((LANES, LANES), 0) // 64) == (_iota((LANES, LANES), 1) // 64)
    pairs = range(HALF_W // LANES)
    state = [s_ref[p] for p in pairs]

    for blk in range(nb):
        rows = slice(blk * ROWS, (blk + 1) * ROWS)
        qs = qs_all[rows]
        kx = kx_all[rows]
        v = i_ref[rows, :]
        b = _dot_exact_lhs(tri, lf_all[rows])
        bref = _chunk_rows(b, CHUNK // 2, CHUNK + CHUNK // 2)
        blast = _chunk_rows(b, CHUNK - 1, 2 * CHUNK - 1)
        q_t = qs * jnp.exp(b - bref)
        k_t = kx * jnp.exp(bref - b)
        q_in = qs * jnp.exp(b)
        k_st = kx * jnp.exp(blast - b)
        e_last = jnp.exp(blast)
        for p in pairs:
            sl = slice(p * LANES, (p + 1) * LANES)
            sc = _dot_nt(_stack_heads(q_t[:, sl]), k_t[:, sl])
            oi = _dot(jnp.where(valid, sc, 0.0), v[:, sl])
            for c in range(2):
                rs = slice(c * CHUNK, (c + 1) * CHUNK)
                o_inter = _dot_nt(q_in[rs, sl], state[p])
                o_intra = (jnp.where(m1, oi[c * 128:c * 128 + 64], 0.0)
                           + jnp.where(m1, 0.0, oi[c * 128 + 64:c * 128 + 128]))
                y_ref[blk * ROWS + c * CHUNK:blk * ROWS + (c + 1) * CHUNK, sl] = o_inter + o_intra
                upd = jnp.where(blockmask, _dot_tn(v[rs, sl], k_st[rs, sl]), 0.0)
                state[p] = state[p] * e_last[c * CHUNK:c * CHUNK + 1, sl] + upd
    for p in pairs:
        s_ref[p] = state[p]

    o = y_ref[...]
    ms = _dot(o * o, _seg_ones(HALF_W, 64)) * (1.0 / 64)
    gate = gt_ref[...]
    y_ref[...] = o * lax.rsqrt(ms + NORM_EPS) * ng_ref[...] * (gate * _sigmoid(gate))


def _hgrn(z_c, lb_logits, norm_g, layer, batch, seq_len, nb=4):
    n = z_c.shape[0]
    rows = nb * ROWS
    nblk = seq_len // rows

    def spec(col):
        return pl.BlockSpec((rows, HALF_W), lambda b, j: (b * nblk + j, col))

    return pl.pallas_call(
        functools.partial(_hgrn_body, layer=layer, nb=nb),
        grid=(batch, nblk),
        in_specs=[spec(c) for c in range(4)] + [_const_spec(lb_logits.shape), _const_spec((1, HALF_W))],
        out_specs=spec(0),
        out_shape=jax.ShapeDtypeStruct((n, HALF_W), F32),
        scratch_shapes=[pltpu.VMEM((HALF_W // LANES, LANES, LANES), F32)],
        compiler_params=_params(("parallel", "arbitrary")),
        name="hgrn",
    )(z_c, z_c, z_c, z_c, lb_logits, norm_g)


def _gdn_body(z_ref, zh_ref, ba_ref, zz_ref, cw_ref, alog_ref, dtb_ref, eb_ref, ea_ref, ng_ref, y_ref, s_ref, *, nb):
    @pl.when(pl.program_id(1) == 0)
    def _():
        s_ref[...] = jnp.zeros_like(s_ref)

    keep = pl.program_id(1) != 0
    ext = _with_halo(z_ref[...], zh_ref[...], keep)
    cw = cw_ref[...]
    seg = _seg_ones(HALF_W, D_HD)
    tri = jnp.where(_chunk_tri(ROWS), 1.0, 0.0).astype(BF16)
    incl = _chunk_tri(ROWS)
    strict = _chunk_tri(ROWS, strict=True)
    heads = range(D_HEADS)
    sls = [slice(h * D_HD, (h + 1) * D_HD) for h in heads]
    eye = jnp.where(_iota((ROWS, ROWS), 0) == _iota((ROWS, ROWS), 1), 1.0, 0.0)
    eye2 = jnp.concatenate([eye, eye], axis=1)
    zero_blk = jnp.zeros((ROWS, D_HD), BF16)
    chunk_row = _iota((ROWS, 2 * D_HD), 0) // CHUNK

    def to_bd(m_rc):
        mb = m_rc.astype(BF16)
        return jnp.concatenate([jnp.concatenate([mb[:, 0:D_HD], zero_blk], axis=1),
                                jnp.concatenate([zero_blk, mb[:, D_HD:2 * D_HD]], axis=1)], axis=0)

    neg_lower, qk, rhs, wq, k_st_t, d_last = {}, {}, {}, {}, {}, {}
    sol, qo, q_eff, kuw = {}, {}, {}, {}

    def prepare(blk):
        rows = slice(blk * ROWS, (blk + 1) * ROWS)
        win = ext[blk * ROWS:blk * ROWS + ROWS + HALO]
        acc = win[HALO:] * cw[D_CONV - 1:D_CONV]
        for j in range(D_CONV - 1):
            acc = acc + _shifted(win, D_CONV - 1 - j) * cw[j:j + 1]
        yield
        qkv = acc * _sigmoid(acc)
        q_raw = qkv[:, 0:HALF_W]
        k_raw = qkv[:, HALF_W:2 * HALF_W]
        v_all = qkv[:, 2 * HALF_W:3 * HALF_W]
        q_all = q_raw * lax.rsqrt(_dot(q_raw * q_raw, seg) + L2_EPS) * (D_HD ** -0.5)
        k_all = k_raw * lax.rsqrt(_dot(k_raw * k_raw, seg) + L2_EPS)
        yield
        ba = ba_ref[rows, :]
        beta_all = _sigmoid(_dot_exact_rhs(ba, eb_ref[...]))
        g_all = -jnp.exp(alog_ref[...]) * _softplus(_dot_exact_rhs(ba, ea_ref[...]) + dtb_ref[...])
        gc_all = _dot_exact_lhs(tri, g_all)
        glast_all = _chunk_rows(gc_all, CHUNK - 1, 2 * CHUNK - 1)
        yield
        for h in heads:
            un = (blk, h)
            gc = gc_all[:, sls[h]]
            glast = glast_all[:, sls[h]]
            diff = gc - gc.T
            decay = jnp.where(incl, jnp.exp(jnp.where(incl, diff, 0.0)), 0.0)
            q = q_all[:, sls[h]]
            k = k_all[:, sls[h]]
            beta = beta_all[:, sls[h]]
            kb = k * beta
            kq = _dot_nt(jnp.concatenate([kb, q], axis=0), k)
            neg_lower[un] = -jnp.where(strict, kq[0:ROWS] * decay, 0.0)
            qk[un] = (kq[ROWS:2 * ROWS] * decay).astype(BF16)
            egc = jnp.exp(gc)
            rhs[un] = jnp.concatenate([v_all[:, sls[h]] * beta, kb * egc], axis=1)
            wq[un] = q * egc
            k_st_t[un] = (k * jnp.exp(glast - gc)).T.astype(BF16)
            d_last[un] = jnp.exp(glast)
            yield

    def solve(blks):
        units = [(blk, h) for blk in blks for h in heads]
        pair_units = [(blk, h) for blk in blks for h in range(0, D_HEADS, 2)]
        t_list = []
        yield from _neumann_steps([jnp.concatenate([neg_lower[(blk, h)], neg_lower[(blk, h + 1)]], axis=1)
                                   for blk, h in pair_units], eye2, to_bd, t_list)
        t_pairs = dict(zip(pair_units, t_list))
        for blk, h in units:
            sol[(blk, h)] = _dot(t_pairs[(blk, h - h % 2)][:, (h % 2) * D_HD:(h % 2 + 1) * D_HD], rhs[(blk, h)])
        yield
        for un in units:
            qo[un] = _dot(qk[un], sol[un])
            q_eff[un] = (wq[un] - qo[un][:, D_HD:2 * D_HD]).astype(BF16)
        yield
        for c in range(2):
            for un in units:
                kuw[un + (c,)] = _dot(k_st_t[un], jnp.where(chunk_row == c, sol[un], 0.0))
            yield

    state = [s_ref[h] for h in heads]

    def sequential(blk):
        for c in range(2):
            rs = slice(c * CHUNK, (c + 1) * CHUNK)
            out_rows = slice(blk * ROWS + c * CHUNK, blk * ROWS + (c + 1) * CHUNK)
            for h in heads:
                un = (blk, h)
                s_b = state[h].astype(BF16)
                o = _dot(q_eff[un][rs], s_b) + qo[un][rs, 0:D_HD]
                k_uw = kuw[un + (c,)]
                state[h] = (state[h] * d_last[un][c * CHUNK:c * CHUNK + 1] + k_uw[:, 0:D_HD]
                            - _dot(k_uw[:, D_HD:2 * D_HD], s_b))
                z = zz_ref[out_rows, sls[h]]
                y_ref[out_rows, sls[h]] = _rms(o, ng_ref[...]) * (z * _sigmoid(z))
            yield

    first, second = list(range(nb // 2)), list(range(nb // 2, nb))

    def each(fn, blks):
        for blk in blks:
            yield from fn(blk)

    _round_robin(each(prepare, first))
    _round_robin(solve(first), each(prepare, second))
    _round_robin(solve(second), each(sequential, first))
    _round_robin(each(sequential, second))
    for h in heads:
        s_ref[h] = state[h]


def _gdn(z_qkv, z_ba, z_z, conv_w, a_log_full, dt_bias_full, e_b, e_a, norm_g, batch, seq_len, nb=4):
    n = z_qkv.shape[0]
    rows = nb * ROWS
    nblk = seq_len // rows
    w3 = 3 * HALF_W

    def tile(width):
        return pl.BlockSpec((rows, width), lambda b, j: (b * nblk + j, 0))

    step = rows // HALO
    halo = pl.BlockSpec((HALO, w3), lambda b, j: (jnp.maximum((b * nblk + j) * step - 1, 0), 0))
    return pl.pallas_call(
        functools.partial(_gdn_body, nb=nb),
        grid=(batch, nblk),
        in_specs=[tile(w3), halo, tile(LANES), tile(HALF_W),
                  _const_spec((D_CONV, w3)), _const_spec((1, HALF_W)), _const_spec((1, HALF_W)),
                  _const_spec((LANES, HALF_W)), _const_spec((LANES, HALF_W)), _const_spec((1, D_HD))],
        out_specs=tile(HALF_W),
        out_shape=jax.ShapeDtypeStruct((n, HALF_W), F32),
        scratch_shapes=[pltpu.VMEM((D_HEADS, D_HD, D_HD), F32)],
        compiler_params=_params(("parallel", "arbitrary")),
        name="gdn",
    )(z_qkv, z_qkv, z_ba, z_z, conv_w, a_log_full, dt_bias_full, e_b, e_a, norm_g)


def _gelu_tanh(x):
    return 0.5 * x * (1.0 + jnp.tanh((2.0 / jnp.pi) ** 0.5 * (x + 0.044715 * (x * x * x))))


def _mix_ffn_body(x_ref, xh_ref, ya_ref, yah_ref, yb_ref, ybh_ref, wo_ref, gmix_ref, gpre_ref,
                  win_ref, cw_ref, cb_ref, wout_ref, gpost_ref, o_ref, *, tiles_per_batch):
    keep = (pl.program_id(0) % tiles_per_batch) != 0
    xe = jnp.concatenate([xh_ref[...], x_ref[...]], axis=0)
    yae = jnp.concatenate([yah_ref[...], ya_ref[...]], axis=0)
    ybe = jnp.concatenate([ybh_ref[...], yb_ref[...]], axis=0)
    mix = _dot(yae, wo_ref[0:HALF_W, :]) + _dot(ybe, wo_ref[HALF_W:2 * HALF_W, :])
    x1 = xe + _rms(mix, gmix_ref[...])
    h = _rms(x1, gpre_ref[...]).astype(BF16)
    cw = cw_ref[...]
    cb = cb_ref[...]
    ffw = D_FF // FF_SPLIT
    y = None
    for j in range(FF_SPLIT):
        cols = slice(j * ffw, (j + 1) * ffw)
        gate = _dot(h, win_ref[:, cols])
        ext = jnp.concatenate([jnp.where(keep, gate[0:HALO], 0.0), gate[HALO:]], axis=0)
        up = _dot(h[HALO:], win_ref[:, D_FF + j * ffw:D_FF + (j + 1) * ffw])
        conv = (ext[HALO:] * cw[2:3, cols] + _shifted(ext, 1) * cw[1:2, cols]
                + _shifted(ext, 2) * cw[0:1, cols] + cb[:, cols])
        part = _dot(_gelu_tanh(conv) * up, wout_ref[cols, :])
        y = part if y is None else y + part
    o_ref[...] = x1[HALO:] + _rms(y, gpost_ref[...])


def _mix_ffn(x, ya, yb, w_o_all, mix_layer, g_mix, g_pre, w_in_all, conv_w, conv_b, w_out_all, g_post, layer,
             seq_len, tm=512):
    n, d = x.shape
    return pl.pallas_call(
        functools.partial(_mix_ffn_body, tiles_per_batch=seq_len // tm),
        grid=(n // tm,),
        in_specs=[_row_spec(tm, d), _halo_spec(tm, d),
                  _row_spec(tm, HALF_W), _halo_spec(tm, HALF_W), _row_spec(tm, HALF_W), _halo_spec(tm, HALF_W),
                  _layer_spec(w_o_all.shape, mix_layer), _const_spec((1, d)), _const_spec((1, d)),
                  _layer_spec(w_in_all.shape, layer), _const_spec((3, D_FF)), _const_spec((1, D_FF)),
                  _layer_spec(w_out_all.shape, layer), _const_spec((1, d))],
        out_specs=_row_spec(tm, d),
        out_shape=jax.ShapeDtypeStruct((n, d), F32),
        compiler_params=_params(("parallel",)),
        name="mix_ffn",
    )(x, x, ya, ya, yb, yb, w_o_all, g_mix, g_pre, w_in_all, conv_w, conv_b, w_out_all, g_post)


def _pad_rows(w, start, total):
    return jnp.zeros((total, w.shape[1]), w.dtype).at[start:start + w.shape[0]].set(w)


def _even_in_weights(ev_w_in, rwkv_vres_down):
    a_cols = 3 * HALF_W + A_DECAY_R + A_AAA_R + A_GATE_R
    n_even, d, _ = ev_w_in.shape
    w = ev_w_in.astype(BF16)
    vres = jnp.concatenate([jnp.zeros((1, d, A_MV_R), BF16), rwkv_vres_down.astype(BF16)], axis=0)
    pad = jnp.zeros((n_even, d, A_LR_W - (a_cols - 3 * HALF_W) - A_MV_R), BF16)
    return jnp.concatenate([w[:, :, 0:a_cols], vres, pad, w[:, :, a_cols:]], axis=2)


def _even_params(e, rwkv_mu, rwkv_w0, rwkv_w_up, rwkv_a0, rwkv_a_up, rwkv_g_up, rwkv_k_k, rwkv_k_a,
                 rwkv_vres_up, rwkv_v0):
    lr_w = A_DECAY_R + A_AAA_R + A_GATE_R
    mu = rwkv_mu[e]
    o_w, o_a, o_g, o_v = 0, A_DECAY_R, A_DECAY_R + A_AAA_R, lr_w
    prm = {
        "mu": mu[None, 0:3 * HALF_W],
        "mul": jnp.concatenate([mu[3 * HALF_W:], jnp.zeros((A_LR_W - lr_w,), F32)])[None],
        "w0": rwkv_w0[e][None],
        "wup": _pad_rows(rwkv_w_up[e], o_w, A_LR_W),
        "a0": rwkv_a0[e][None],
        "aup": _pad_rows(rwkv_a_up[e], o_a, A_LR_W),
        "gup": _pad_rows(rwkv_g_up[e], o_g, A_LR_W),
        "k_k": rwkv_k_k[e][None],
        "k_a": rwkv_k_a[e][None],
    }
    if e > 0:
        prm["v0"] = rwkv_v0[e - 1][None]
        prm["vup"] = _pad_rows(rwkv_vres_up[e - 1], o_v, A_LR_W)
    return prm


def _odd_in_weights(od_w_in):
    return jnp.pad(od_w_in.astype(BF16), ((0, 0), (0, 0), (0, LANES - 2 * D_HEADS)))


def _head_expand(offset):
    r = jnp.arange(LANES)[:, None]
    c = jnp.arange(HALF_W)[None, :]
    return (r == offset + c // D_HD).astype(BF16)


def kernel(x, norm_mix_pre, norm_mix_post, norm_ffn_pre, norm_ffn_post, ev_w_in, ev_w_out, rwkv_mu, rwkv_w0, rwkv_w_up, rwkv_a0, rwkv_a_up, rwkv_g_up, rwkv_k_k, rwkv_k_a, rwkv_r_k, rwkv_ln_g, rwkv_ln_b, rwkv_vres_down, rwkv_vres_up, rwkv_v0, sgu_ln_g, sgu_ln_b, sgu_w, sgu_b, od_w_in, od_w_out, hgrn_lb_logits, hgrn_norm_g, gdn_conv_w, gdn_a_log, gdn_dt_bias, gdn_norm_g, ffn_w_in, ffn_conv_w, ffn_conv_b, ffn_w_out):
    batch, seq_len, d = x.shape
    depth = norm_mix_pre.shape[0]
    n = batch * seq_len
    xf = x.reshape(n, d)
    ev_w_in_p = _even_in_weights(ev_w_in, rwkv_vres_down)
    od_w_in_p = _odd_in_weights(od_w_in)
    ev_w_out_b = ev_w_out.astype(BF16)
    od_w_out_b = od_w_out.astype(BF16)
    ffn_w_in_b = ffn_w_in.astype(BF16)
    ffn_w_out_b = ffn_w_out.astype(BF16)
    v_first = None
    for l in range(depth):
        if l % 2 == 0:
            e = l // 2
            prm = _even_params(e, rwkv_mu, rwkv_w0, rwkv_w_up, rwkv_a0, rwkv_a_up, rwkv_g_up, rwkv_k_k, rwkv_k_a,
                               rwkv_vres_up, rwkv_v0)
            z_rkv, z_lr, z_b = _inproj(xf, norm_mix_pre[l][None], ev_w_in_p, e, (3 * HALF_W, A_LR_W, 2 * HALF_W))
            ya, v_a = _rwkv(z_rkv, z_lr, v_first if e > 0 else None, prm, rwkv_ln_g[e][None], rwkv_ln_b[e][None],
                            rwkv_r_k[e].reshape(1, HALF_W), batch, seq_len)
            if e == 0:
                v_first = v_a
            w_s = sgu_w[e]
            w_pairs = jnp.concatenate([w_s[0::2], w_s[1::2]], axis=2)
            bias = jnp.repeat(sgu_b[e].T, HALF_W // B_GROUPS, axis=1)
            yb = _sgu(z_b, sgu_ln_g[e][None], sgu_ln_b[e][None], w_pairs, bias)
            w_out_all, mix_layer = ev_w_out_b, e
        else:
            o = l // 2
            z_c, z_qkv, z_z, z_ba = _inproj(xf, norm_mix_pre[l][None], od_w_in_p, o,
                                            (4 * HALF_W, 3 * HALF_W, HALF_W, LANES))
            ya = _hgrn(z_c, hgrn_lb_logits, hgrn_norm_g[o][None], o, batch, seq_len)
            yb = _gdn(z_qkv, z_ba, z_z, gdn_conv_w[o], jnp.repeat(gdn_a_log[o], D_HD)[None],
                      jnp.repeat(gdn_dt_bias[o], D_HD)[None], _head_expand(0), _head_expand(D_HEADS),
                      gdn_norm_g[o][None], batch, seq_len)
            w_out_all, mix_layer = od_w_out_b, o
        xf = _mix_ffn(xf, ya, yb, w_out_all, mix_layer, norm_mix_post[l][None], norm_ffn_pre[l][None],
                      ffn_w_in_b, ffn_conv_w[l], ffn_conv_b[l][None], ffn_w_out_b, norm_ffn_post[l][None], l,
                      seq_len)
    return xf.reshape(batch, seq_len, d)
```

```python
import functools

import jax
import jax.numpy as jnp
from jax import lax
from jax.experimental import pallas as pl
from jax.experimental.pallas import tpu as pltpu

F32 = jnp.float32
BF16 = jnp.bfloat16

D_MODEL = 1024
CHUNK = 64
HALF_W = 512
A_DECAY_R = 32
A_AAA_R = 32
A_MV_R = 32
A_GATE_R = 96
A_LR_W = 256
SGU_BLOCK = 128
B_GROUPS = 8
D_HEADS = 4
D_HD = 128
D_CONV = 4
D_FF = 2816
NORM_EPS = 1e-6
RWKV_LN_EPS = 64e-5
SGU_LN_EPS = 1e-5
L2_EPS = 1e-6

ROWS = 128
HALO = 16
FF_SPLIT = 1
LANES = 128
VMEM_LIMIT = 56 * 1024 * 1024


def _dot(a, b):
    return jnp.dot(a.astype(BF16), b.astype(BF16), preferred_element_type=F32)


def _dot_nt(a, b):
    return lax.dot_general(a.astype(BF16), b.astype(BF16), (((1,), (1,)), ((), ())), preferred_element_type=F32)


def _dot_tn(a, b):
    return lax.dot_general(a.astype(BF16), b.astype(BF16), (((0,), (0,)), ((), ())), preferred_element_type=F32)


def _dot_exact_lhs(m_bf16, x):
    hi = x.astype(BF16)
    lo = (x - hi.astype(F32)).astype(BF16)
    return _dot(m_bf16, hi) + _dot(m_bf16, lo)


def _dot_exact_rhs(x, m_bf16):
    hi = x.astype(BF16)
    lo = (x - hi.astype(F32)).astype(BF16)
    return _dot(hi, m_bf16) + _dot(lo, m_bf16)


def _sigmoid(x):
    return 1.0 / (1.0 + jnp.exp(-x))


def _softplus(x):
    return jnp.maximum(x, 0.0) + jnp.log(1.0 + jnp.exp(-jnp.abs(x)))


def _rms(x, g):
    return x * lax.rsqrt(jnp.mean(x * x, axis=-1, keepdims=True) + NORM_EPS) * g


def _iota(shape, dim):
    return lax.broadcasted_iota(jnp.int32, shape, dim)


def _chunk_tri(n, strict=False):
    r = _iota((n, n), 0)
    c = _iota((n, n), 1)
    same = (r // CHUNK) == (c // CHUNK)
    return same & ((c < r) if strict else (c <= r))


def _seg_ones(width, seg):
    r = _iota((width, width), 0)
    c = _iota((width, width), 1)
    return jnp.where((r // seg) == (c // seg), 1.0, 0.0).astype(BF16)


def _stack_heads(x):
    lane = _iota((CHUNK, LANES), 1)
    m1 = lane < 64
    top, bot = x[0:CHUNK], x[CHUNK:2 * CHUNK]
    return jnp.concatenate([jnp.where(m1, top, 0.0), jnp.where(m1, 0.0, top),
                            jnp.where(m1, bot, 0.0), jnp.where(m1, 0.0, bot)], axis=0)


def _neumann_steps(n_mats, eye, to_bd, out):
    ps = [eye + n for n in n_mats]
    ms = list(n_mats)
    bds = [to_bd(m) for m in ms]
    for _ in range(5):
        ms = [_dot(m, bd) for m, bd in zip(ms, bds)]
        yield
        bds = [to_bd(m) for m in ms]
        ps = [p + _dot(p, bd) for p, bd in zip(ps, bds)]
        yield
    out.extend(ps)


def _round_robin(*gens):
    gens = list(gens)
    while gens:
        for g in list(gens):
            try:
                next(g)
            except StopIteration:
                gens.remove(g)


def _chunk_rows(x, r0, r1):
    w = x.shape[1]
    return jnp.concatenate([jnp.broadcast_to(x[r0:r0 + 1], (CHUNK, w)),
                            jnp.broadcast_to(x[r1:r1 + 1], (CHUNK, w))], axis=0)


def _with_halo(x, halo, keep):
    return jnp.concatenate([jnp.where(keep, halo, 0.0), x], axis=0)


def _shifted(ext, s):
    if s == 0:
        return ext[HALO:]
    return pltpu.roll(ext, s, 0)[HALO:]


def _params(sem):
    return pltpu.CompilerParams(dimension_semantics=sem, vmem_limit_bytes=VMEM_LIMIT)


def _layer_spec(shape, layer):
    tail = (0,) * (len(shape) - 1)
    return pl.BlockSpec((None,) + tuple(shape[1:]), lambda *_: (layer,) + tail, pipeline_mode=pl.Buffered(1))


def _row_spec(tm, width, col=0):
    return pl.BlockSpec((tm, width), lambda i, col=col: (i, col))


def _halo_spec(tm, width, col=0):
    step = tm // HALO
    return pl.BlockSpec((HALO, width), lambda i, col=col: (jnp.maximum(i * step - 1, 0), col))


def _const_spec(shape):
    return pl.BlockSpec(shape, lambda *_: (0,) * len(shape), pipeline_mode=pl.Buffered(1))


def _inproj_body(x_ref, g_ref, w_ref, *o_refs, widths):
    h = _rms(x_ref[...], g_ref[...]).astype(BF16)
    off = 0
    for o_ref, wd in zip(o_refs, widths):
        o_ref[...] = _dot(h, w_ref[:, off:off + wd])
        off += wd


def _inproj(x, g_all, g_layer, w_all, layer, widths, tm=512):
    n, d = x.shape
    return pl.pallas_call(
        functools.partial(_inproj_body, widths=widths),
        grid=(n // tm,),
        in_specs=[_row_spec(tm, d), _layer_spec(g_all.shape, g_layer), _layer_spec(w_all.shape, layer)],
        out_specs=[_row_spec(tm, wd) for wd in widths],
        out_shape=[jax.ShapeDtypeStruct((n, wd), F32) for wd in widths],
        compiler_params=_params(("parallel",)),
        name="inproj",
    )(x, g_all, w_all)


def _rwkv_body(*refs, has_vres, nb):
    if has_vres:
        (z_ref, zh_ref, l_ref, lh_ref, vf_ref, mu_ref, mul_ref, w0_ref, wup_ref, a0_ref, aup_ref, gup_ref,
         kk_ref, ka_ref, v0_ref, vup_ref, lng_ref, lnb_ref, rk_ref, y_ref, s_ref) = refs
    else:
        (z_ref, zh_ref, l_ref, lh_ref, mu_ref, mul_ref, w0_ref, wup_ref, a0_ref, aup_ref, gup_ref,
         kk_ref, ka_ref, lng_ref, lnb_ref, rk_ref, y_ref, vout_ref, s_ref) = refs

    @pl.when(pl.program_id(1) == 0)
    def _():
        s_ref[...] = jnp.zeros_like(s_ref)

    keep = pl.program_id(1) != 0
    z = z_ref[...]
    za_all = z + mu_ref[...] * (_shifted(_with_halo(z, zh_ref[...], keep), 1) - z)
    zl = l_ref[...]
    zl_all = zl + mul_ref[...] * (_shifted(_with_halo(zl, lh_ref[...], keep), 1) - zl)
    seg = _seg_ones(HALF_W, 64)

    pairs = range(HALF_W // LANES)
    sls = [slice(p * LANES, (p + 1) * LANES) for p in pairs]
    tri = jnp.where(_chunk_tri(ROWS), 1.0, 0.0).astype(BF16)
    rr = _iota((ROWS, 2 * ROWS), 0)
    cc = _iota((ROWS, 2 * ROWS), 1)
    same_chunk = (rr // CHUNK) == (cc // ROWS)
    rc_strict = same_chunk & ((cc % CHUNK) < (rr % CHUNK))
    rc_incl = same_chunk & ((cc % CHUNK) <= (rr % CHUNK))

    head_blk = (_iota((LANES, LANES), 0) // CHUNK) == (_iota((LANES, LANES), 1) // CHUNK)
    eye_c = jnp.where((_iota((CHUNK, LANES), 1) % CHUNK) == _iota((CHUNK, LANES), 0), 1.0, 0.0)

    def to_bd_c(m_c):
        return jnp.where(head_blk, jnp.concatenate([m_c, m_c], axis=0), 0.0).astype(BF16)
    q, v_s, a_in_s, r_in, bk_t, e_col, post_in = {}, {}, {}, {}, {}, {}, {}
    a_rb, av, tt, r_eff, y_off, mb = {}, {}, {}, {}, {}, {}
    zero_blk = jnp.zeros((LANES, LANES), BF16)

    def prepare(blk):
        rows = slice(blk * ROWS, (blk + 1) * ROWS)
        za = za_all[rows]
        zl = zl_all[rows]
        r = za[:, 0:HALF_W]
        k_raw = za[:, HALF_W:2 * HALF_W]
        v = za[:, 2 * HALF_W:3 * HALF_W]
        if has_vres:
            gate = _sigmoid(v0_ref[...] + _dot(zl, vup_ref[...]))
            v = v + (vf_ref[rows, :] - v) * gate
        else:
            vout_ref[rows, :] = v
        yield
        w_log = -_softplus(-(w0_ref[...] + _dot(jnp.tanh(zl), wup_ref[...]))) - 0.5
        lw = -jnp.exp(w_log)
        a = _sigmoid(a0_ref[...] + _dot(zl, aup_ref[...]))
        g = _dot(_sigmoid(zl), gup_ref[...])
        yield
        kkk = k_raw * kk_ref[...]
        kk = kkk * lax.rsqrt(_dot(kkk * kkk, seg) + L2_EPS)
        k = k_raw * (1.0 + (a - 1.0) * ka_ref[...])
        bonus = _dot_exact_rhs(r * k * rk_ref[...], seg)
        post_in[blk] = (bonus * v, g)
        yield
        b = _dot_exact_lhs(tri, lw)
        bref = _chunk_rows(b, CHUNK // 2, CHUNK + CHUNK // 2)
        blast = _chunk_rows(b, CHUNK - 1, 2 * CHUNK - 1)
        e_pos = jnp.exp(b - bref)
        e_neg = jnp.exp(bref - b)
        e_in = jnp.exp(b)
        e_st = jnp.exp(blast - b)
        e_last = jnp.exp(blast)
        yield
        kka = kk * a
        neg_kk_exc = -(kk * jnp.exp(-lw))
        a_t = neg_kk_exc * e_pos
        a_in = neg_kk_exc * e_in
        r_t = r * e_pos
        r_in_blk = r * e_in
        b_t = kka * e_neg
        b_st = kka * e_st
        k_t = k * e_neg
        k_st = k * e_st
        for p in pairs:
            sl = sls[p]
            un = (blk, p)
            q[un] = _dot_nt(jnp.concatenate([a_t[:, sl], r_t[:, sl]], axis=0),
                            jnp.concatenate([_stack_heads(b_t[:, sl]), _stack_heads(k_t[:, sl])], axis=0))
            v_s[un] = _stack_heads(v[:, sl]).astype(BF16)
            a_in_s[un] = _stack_heads(a_in[:, sl])
            r_in[un] = r_in_blk[:, sl]
            b_s = _stack_heads(b_st[:, sl])
            k_s = _stack_heads(k_st[:, sl])
            for c in range(2):
                ss = slice(c * LANES, (c + 1) * LANES)
                bk_t[un + (c,)] = jnp.concatenate([b_s[ss].T, k_s[ss].T], axis=1).astype(BF16)
                e_col[un + (c,)] = jnp.broadcast_to(e_last[c * CHUNK:c * CHUNK + 1, sl], (LANES, LANES)).T
            yield

    def solve(units):
        a_kk = {}
        for u in units:
            a_rb[u] = jnp.where(rc_incl, q[u][ROWS:2 * ROWS, 0:256], 0.0).astype(BF16)
            a_kk[u] = jnp.concatenate([jnp.where(rc_strict, q[u][0:ROWS, 256:512], 0.0),
                                       jnp.where(rc_incl, q[u][ROWS:2 * ROWS, 256:512], 0.0)], axis=0)
        yield
        n_list = [jnp.where(rc_strict, q[u][0:ROWS, 0:256], 0.0)[c * CHUNK:(c + 1) * CHUNK, c * LANES:(c + 1) * LANES]
                  for u in units for c in range(2)]
        t_list = []
        yield from _neumann_steps(n_list, eye_c, to_bd_c, t_list)
        for u in units:
            av[u] = _dot(a_kk[u], v_s[u])
        yield
        for i, u in enumerate(units):
            x_s = jnp.concatenate([a_in_s[u], _stack_heads(av[u][0:ROWS])], axis=1).astype(BF16)
            tt[u] = jnp.concatenate([_dot(t_list[2 * i + c], x_s[c * LANES:(c + 1) * LANES]) for c in range(2)],
                                    axis=0)
        yield
        wu_s = {u: jnp.concatenate([_stack_heads(tt[u][:, 0:LANES]), _stack_heads(tt[u][:, LANES:2 * LANES])],
                                   axis=1).astype(BF16) for u in units}
        for u in units:
            ru = _dot(a_rb[u], wu_s[u])
            r_eff[u] = (r_in[u] + ru[:, 0:LANES]).astype(BF16)
            y_off[u] = ru[:, LANES:2 * LANES] + av[u][ROWS:2 * ROWS]
        yield
        for c in range(2):
            ss = slice(c * LANES, (c + 1) * LANES)
            for u in units:
                rhs = jnp.concatenate([wu_s[u][ss], jnp.concatenate([zero_blk, v_s[u][ss]], axis=1)], axis=0)
                mb[u + (c,)] = _dot(bk_t[u + (c,)], rhs)
            yield

    state = [s_ref[p] for p in pairs]

    def sequential(blk):
        rows = slice(blk * ROWS, (blk + 1) * ROWS)
        for c in range(2):
            rs = slice(c * CHUNK, (c + 1) * CHUNK)
            out_rows = slice(blk * ROWS + c * CHUNK, blk * ROWS + (c + 1) * CHUNK)
            for p in pairs:
                un = (blk, p)
                h_b = state[p].astype(BF16)
                y_ref[out_rows, sls[p]] = _dot(r_eff[un][rs], h_b) + y_off[un][rs]
                m_b = mb[un + (c,)]
                state[p] = state[p] * e_col[un + (c,)] + _dot(m_b[:, 0:LANES], h_b) + m_b[:, LANES:2 * LANES]
            yield
        y = y_ref[rows, :]
        mu_y = _dot_exact_rhs(y, seg) * (1.0 / 64)
        d = y - mu_y
        var = _dot(d * d, seg) * (1.0 / 64)
        bonus_v, g = post_in[blk]
        y_ref[rows, :] = (d * lax.rsqrt(var + RWKV_LN_EPS) * lng_ref[...] + lnb_ref[...] + bonus_v) * g
        yield

    first, second = list(range(nb // 2)), list(range(nb // 2, nb))

    def each(fn, blks):
        for blk in blks:
            yield from fn(blk)

    _round_robin(each(prepare, first))
    _round_robin(solve([(blk, p) for blk in first for p in pairs]), each(prepare, second))
    _round_robin(solve([(blk, p) for blk in second for p in pairs]), each(sequential, first))
    _round_robin(each(sequential, second))
    for p in pairs:
        s_ref[p] = state[p]


def _rwkv(z_rkv, z_lr, v_first, prm, e, batch, seq_len, nb=4):
    n = z_rkv.shape[0]
    has_vres = v_first is not None
    rows = nb * ROWS
    nblk = seq_len // rows
    w3 = 3 * HALF_W

    def tile(width):
        return pl.BlockSpec((rows, width), lambda b, j: (b * nblk + j, 0))

    def halo(width):
        step = rows // HALO
        return pl.BlockSpec((HALO, width), lambda b, j: (jnp.maximum((b * nblk + j) * step - 1, 0), 0))

    ins = [z_rkv, z_rkv, z_lr, z_lr]
    specs = [tile(w3), halo(w3), tile(A_LR_W), halo(A_LR_W)]
    if has_vres:
        ins.append(v_first)
        specs.append(tile(HALF_W))
    names = (["mu", "mul", "w0", "wup", "a0", "aup", "gup", "k_k", "k_a"] + (["v0", "vup"] if has_vres else [])
             + ["ln_g", "ln_b", "r_k"])
    for nm in names:
        ins.append(prm[nm])
        specs.append(_layer_spec(prm[nm].shape, e - 1 if nm in ("v0", "vup") else e))
    n_out = 1 if has_vres else 2
    out = pl.pallas_call(
        functools.partial(_rwkv_body, has_vres=has_vres, nb=nb),
        grid=(batch, nblk),
        in_specs=specs,
        out_specs=[tile(HALF_W)] * n_out,
        out_shape=[jax.ShapeDtypeStruct((n, HALF_W), F32)] * n_out,
        scratch_shapes=[pltpu.VMEM((HALF_W // LANES, LANES, LANES), F32)],
        compiler_params=_params(("parallel", "arbitrary")),
        name="rwkv",
    )(*ins)
    return (out[0], None) if has_vres else (out[0], out[1])


def _gelu_erf(x):
    return 0.5 * x * (1.0 + lax.erf(x * (2.0 ** -0.5)))


def _sgu_body(u_ref, v_ref, lng_ref, lnb_ref, w_ref, bias_ref, o_ref, *, tm):
    seg = _seg_ones(HALF_W, 64)
    u = _gelu_erf(u_ref[...])
    v = _gelu_erf(v_ref[...])
    mu = _dot_exact_rhs(v, seg) * (1.0 / 64)
    d = v - mu
    var = _dot(d * d, seg) * (1.0 / 64)
    vn = d * lax.rsqrt(var + SGU_LN_EPS) * lng_ref[...] + lnb_ref[...]
    r = _iota((SGU_BLOCK, 2 * SGU_BLOCK), 0)
    c = _iota((SGU_BLOCK, 2 * SGU_BLOCK), 1)
    causal = (r // CHUNK) >= ((c % SGU_BLOCK) // CHUNK)
    lane = _iota((SGU_BLOCK, LANES), 1)
    m1 = lane < 64
    bias = bias_ref[...]
    for p in range(HALF_W // LANES):
        sl = slice(p * LANES, (p + 1) * LANES)
        w = jnp.where(causal, w_ref[p], 0.0)
        for nb in range(tm // SGU_BLOCK):
            rs = slice(nb * SGU_BLOCK, (nb + 1) * SGU_BLOCK)
            blk = vn[rs, sl]
            stacked = jnp.concatenate([jnp.where(m1, blk, 0.0), jnp.where(m1, 0.0, blk)], axis=0)
            mixed = _dot(w, stacked) + bias[:, sl]
            o_ref[rs, sl] = u[rs, sl] * mixed


def _sgu(z_b, prm, e, tm=512):
    n = z_b.shape[0]
    names = ["sgu_ln_g", "sgu_ln_b", "sgu_w", "sgu_bias"]
    return pl.pallas_call(
        functools.partial(_sgu_body, tm=tm),
        grid=(n // tm,),
        in_specs=[_row_spec(tm, HALF_W, 0), _row_spec(tm, HALF_W, 1)]
        + [_layer_spec(prm[nm].shape, e) for nm in names],
        out_specs=_row_spec(tm, HALF_W),
        out_shape=jax.ShapeDtypeStruct((n, HALF_W), F32),
        compiler_params=_params(("parallel",)),
        name="sgu",
    )(z_b, z_b, *[prm[nm] for nm in names])


def _hgrn_body(q_ref, f_ref, i_ref, gt_ref, lbl_ref, ng_ref, y_ref, s_ref, *, layer, nb):
    @pl.when(pl.program_id(1) == 0)
    def _():
        s_ref[...] = jnp.zeros_like(s_ref)

    logits = lbl_ref[...]
    e = jnp.exp(logits - jnp.max(logits, axis=0, keepdims=True))
    prob = e / jnp.sum(e, axis=0, keepdims=True)
    lb = jnp.sum(prob[0:layer + 1], axis=0, keepdims=True) - prob[0:1]

    f = lb + (1.0 - lb) * _sigmoid(f_ref[...])
    lf_all = jnp.log(f)
    kx_all = 1.0 - f
    q = q_ref[...]
    qs_all = q * _sigmoid(q)
    tri = jnp.where(_chunk_tri(ROWS), 1.0, 0.0).astype(BF16)
    rr = _iota((2 * ROWS, ROWS), 0)
    cc = _iota((2 * ROWS, ROWS), 1)
    valid = ((rr // ROWS) == (cc // CHUNK)) & ((cc % CHUNK) <= (rr % CHUNK))
    m1 = _iota((CHUNK, LANES), 1) < 64
    blockmask = (_iota((LANES, LANES), 0) // 64) == (_iota((LANES, LANES), 1) // 64)
    pairs = range(HALF_W // LANES)
    state = [s_ref[p] for p in pairs]

    for blk in range(nb):
        rows = slice(blk * ROWS, (blk + 1) * ROWS)
        qs = qs_all[rows]
        kx = kx_all[rows]
        v = i_ref[rows, :]
        b = _dot_exact_lhs(tri, lf_all[rows])
        bref = _chunk_rows(b, CHUNK // 2, CHUNK + CHUNK // 2)
        blast = _chunk_rows(b, CHUNK - 1, 2 * CHUNK - 1)
        q_t = qs * jnp.exp(b - bref)
        k_t = kx * jnp.exp(bref - b)
        q_in = qs * jnp.exp(b)
        k_st = kx * jnp.exp(blast - b)
        e_last = jnp.exp(blast)
        for p in pairs:
            sl = slice(p * LANES, (p + 1) * LANES)
            sc = _dot_nt(_stack_heads(q_t[:, sl]), k_t[:, sl])
            oi = _dot(jnp.where(valid, sc, 0.0), v[:, sl])
            for c in range(2):
                rs = slice(c * CHUNK, (c + 1) * CHUNK)
                o_inter = _dot_nt(q_in[rs, sl], state[p])
                o_intra = (jnp.where(m1, oi[c * 128:c * 128 + 64], 0.0)
                           + jnp.where(m1, 0.0, oi[c * 128 + 64:c * 128 + 128]))
                y_ref[blk * ROWS + c * CHUNK:blk * ROWS + (c + 1) * CHUNK, sl] = o_inter + o_intra
                upd = jnp.where(blockmask, _dot_tn(v[rs, sl], k_st[rs, sl]), 0.0)
                state[p] = state[p] * e_last[c * CHUNK:c * CHUNK + 1, sl] + upd
    for p in pairs:
        s_ref[p] = state[p]

    o = y_ref[...]
    ms = _dot(o * o, _seg_ones(HALF_W, 64)) * (1.0 / 64)
    gate = gt_ref[...]
    y_ref[...] = o * lax.rsqrt(ms + NORM_EPS) * ng_ref[...] * (gate * _sigmoid(gate))


def _hgrn(z_c, lb_logits, norm_g_all, layer, batch, seq_len, nb=4):
    n = z_c.shape[0]
    rows = nb * ROWS
    nblk = seq_len // rows

    def spec(col):
        return pl.BlockSpec((rows, HALF_W), lambda b, j: (b * nblk + j, col))

    return pl.pallas_call(
        functools.partial(_hgrn_body, layer=layer, nb=nb),
        grid=(batch, nblk),
        in_specs=[spec(c) for c in range(4)] + [_const_spec(lb_logits.shape), _layer_spec(norm_g_all.shape, layer)],
        out_specs=spec(0),
        out_shape=jax.ShapeDtypeStruct((n, HALF_W), F32),
        scratch_shapes=[pltpu.VMEM((HALF_W // LANES, LANES, LANES), F32)],
        compiler_params=_params(("parallel", "arbitrary")),
        name="hgrn",
    )(z_c, z_c, z_c, z_c, lb_logits, norm_g_all)


def _gdn_body(z_ref, zh_ref, ba_ref, zz_ref, cw_ref, alog_ref, dtb_ref, eb_ref, ea_ref, ng_ref, y_ref, s_ref, *, nb):
    @pl.when(pl.program_id(1) == 0)
    def _():
        s_ref[...] = jnp.zeros_like(s_ref)

    keep = pl.program_id(1) != 0
    ext = _with_halo(z_ref[...], zh_ref[...], keep)
    cw = cw_ref[...]
    seg = _seg_ones(HALF_W, D_HD)
    tri = jnp.where(_chunk_tri(ROWS), 1.0, 0.0).astype(BF16)
    incl = _chunk_tri(ROWS)
    strict = _chunk_tri(ROWS, strict=True)
    heads = range(D_HEADS)
    sls = [slice(h * D_HD, (h + 1) * D_HD) for h in heads]
    eye = jnp.where(_iota((ROWS, ROWS), 0) == _iota((ROWS, ROWS), 1), 1.0, 0.0)
    eye2 = jnp.concatenate([eye, eye], axis=1)
    zero_blk = jnp.zeros((ROWS, D_HD), BF16)
    chunk_row = _iota((ROWS, 2 * D_HD), 0) // CHUNK

    def to_bd(m_rc):
        mb = m_rc.astype(BF16)
        return jnp.concatenate([jnp.concatenate([mb[:, 0:D_HD], zero_blk], axis=1),
                                jnp.concatenate([zero_blk, mb[:, D_HD:2 * D_HD]], axis=1)], axis=0)

    neg_lower, qk, rhs, wq, k_st_t, d_last = {}, {}, {}, {}, {}, {}
    sol, qo, q_eff, kuw = {}, {}, {}, {}

    def prepare(blk):
        rows = slice(blk * ROWS, (blk + 1) * ROWS)
        win = ext[blk * ROWS:blk * ROWS + ROWS + HALO]
        acc = win[HALO:] * cw[D_CONV - 1:D_CONV]
        for j in range(D_CONV - 1):
            acc = acc + _shifted(win, D_CONV - 1 - j) * cw[j:j + 1]
        yield
        qkv = acc * _sigmoid(acc)
        q_raw = qkv[:, 0:HALF_W]
        k_raw = qkv[:, HALF_W:2 * HALF_W]
        v_all = qkv[:, 2 * HALF_W:3 * HALF_W]
        q_all = q_raw * lax.rsqrt(_dot(q_raw * q_raw, seg) + L2_EPS) * (D_HD ** -0.5)
        k_all = k_raw * lax.rsqrt(_dot(k_raw * k_raw, seg) + L2_EPS)
        yield
        ba = ba_ref[rows, :]
        beta_all = _sigmoid(_dot_exact_rhs(ba, eb_ref[...]))
        g_all = -jnp.exp(alog_ref[...]) * _softplus(_dot_exact_rhs(ba, ea_ref[...]) + dtb_ref[...])
        gc_all = _dot_exact_lhs(tri, g_all)
        glast_all = _chunk_rows(gc_all, CHUNK - 1, 2 * CHUNK - 1)
        yield
        for h in heads:
            un = (blk, h)
            gc = gc_all[:, sls[h]]
            glast = glast_all[:, sls[h]]
            diff = gc - gc.T
            decay = jnp.where(incl, jnp.exp(jnp.where(incl, diff, 0.0)), 0.0)
            q = q_all[:, sls[h]]
            k = k_all[:, sls[h]]
            beta = beta_all[:, sls[h]]
            kb = k * beta
            kq = _dot_nt(jnp.concatenate([kb, q], axis=0), k)
            neg_lower[un] = -jnp.where(strict, kq[0:ROWS] * decay, 0.0)
            qk[un] = (kq[ROWS:2 * ROWS] * decay).astype(BF16)
            egc = jnp.exp(gc)
            rhs[un] = jnp.concatenate([v_all[:, sls[h]] * beta, kb * egc], axis=1)
            wq[un] = q * egc
            k_st_t[un] = (k * jnp.exp(glast - gc)).T.astype(BF16)
            d_last[un] = jnp.exp(glast)
            yield

    def solve(blks):
        units = [(blk, h) for blk in blks for h in heads]
        pair_units = [(blk, h) for blk in blks for h in range(0, D_HEADS, 2)]
        t_list = []
        yield from _neumann_steps([jnp.concatenate([neg_lower[(blk, h)], neg_lower[(blk, h + 1)]], axis=1)
                                   for blk, h in pair_units], eye2, to_bd, t_list)
        t_pairs = dict(zip(pair_units, t_list))
        for blk, h in units:
            sol[(blk, h)] = _dot(t_pairs[(blk, h - h % 2)][:, (h % 2) * D_HD:(h % 2 + 1) * D_HD], rhs[(blk, h)])
        yield
        for un in units:
            qo[un] = _dot(qk[un], sol[un])
            q_eff[un] = (wq[un] - qo[un][:, D_HD:2 * D_HD]).astype(BF16)
        yield
        for c in range(2):
            for un in units:
                kuw[un + (c,)] = _dot(k_st_t[un], jnp.where(chunk_row == c, sol[un], 0.0))
            yield

    state = [s_ref[h] for h in heads]

    def sequential(blk):
        for c in range(2):
            rs = slice(c * CHUNK, (c + 1) * CHUNK)
            out_rows = slice(blk * ROWS + c * CHUNK, blk * ROWS + (c + 1) * CHUNK)
            for h in heads:
                un = (blk, h)
                s_b = state[h].astype(BF16)
                o = _dot(q_eff[un][rs], s_b) + qo[un][rs, 0:D_HD]
                k_uw = kuw[un + (c,)]
                state[h] = (state[h] * d_last[un][c * CHUNK:c * CHUNK + 1] + k_uw[:, 0:D_HD]
                            - _dot(k_uw[:, D_HD:2 * D_HD], s_b))
                z = zz_ref[out_rows, sls[h]]
                y_ref[out_rows, sls[h]] = _rms(o, ng_ref[...]) * (z * _sigmoid(z))
            yield

    first, second = list(range(nb // 2)), list(range(nb // 2, nb))

    def each(fn, blks):
        for blk in blks:
            yield from fn(blk)

    _round_robin(each(prepare, first))
    _round_robin(solve(first), each(prepare, second))
    _round_robin(solve(second), each(sequential, first))
    _round_robin(each(sequential, second))
    for h in heads:
        s_ref[h] = state[h]


def _gdn(z_qkv, z_ba, z_z, conv_w_all, a_log_all, dt_bias_all, e_b, e_a, norm_g_all, layer, batch, seq_len, nb=4):
    n = z_qkv.shape[0]
    rows = nb * ROWS
    nblk = seq_len // rows
    w3 = 3 * HALF_W

    def tile(width):
        return pl.BlockSpec((rows, width), lambda b, j: (b * nblk + j, 0))

    step = rows // HALO
    halo = pl.BlockSpec((HALO, w3), lambda b, j: (jnp.maximum((b * nblk + j) * step - 1, 0), 0))
    return pl.pallas_call(
        functools.partial(_gdn_body, nb=nb),
        grid=(batch, nblk),
        in_specs=[tile(w3), halo, tile(LANES), tile(HALF_W),
                  _layer_spec(conv_w_all.shape, layer), _layer_spec(a_log_all.shape, layer),
                  _layer_spec(dt_bias_all.shape, layer),
                  _const_spec((LANES, HALF_W)), _const_spec((LANES, HALF_W)), _layer_spec(norm_g_all.shape, layer)],
        out_specs=tile(HALF_W),
        out_shape=jax.ShapeDtypeStruct((n, HALF_W), F32),
        scratch_shapes=[pltpu.VMEM((D_HEADS, D_HD, D_HD), F32)],
        compiler_params=_params(("parallel", "arbitrary")),
        name="gdn",
    )(z_qkv, z_qkv, z_ba, z_z, conv_w_all, a_log_all, dt_bias_all, e_b, e_a, norm_g_all)


def _gelu_tanh(x):
    return 0.5 * x * (1.0 + jnp.tanh((2.0 / jnp.pi) ** 0.5 * (x + 0.044715 * (x * x * x))))


def _mix_ffn_body(x_ref, xh_ref, ya_ref, yah_ref, yb_ref, ybh_ref, wo_ref, gmix_ref, gpre_ref,
                  win_ref, cw_ref, cb_ref, wout_ref, gpost_ref, o_ref, *, tiles_per_batch):
    keep = (pl.program_id(0) % tiles_per_batch) != 0
    xe = jnp.concatenate([xh_ref[...], x_ref[...]], axis=0)
    yae = jnp.concatenate([yah_ref[...], ya_ref[...]], axis=0)
    ybe = jnp.concatenate([ybh_ref[...], yb_ref[...]], axis=0)
    mix = _dot(yae, wo_ref[0:HALF_W, :]) + _dot(ybe, wo_ref[HALF_W:2 * HALF_W, :])
    x1 = xe + _rms(mix, gmix_ref[...])
    h = _rms(x1, gpre_ref[...]).astype(BF16)
    cw = cw_ref[...]
    cb = cb_ref[...]
    ffw = D_FF // FF_SPLIT
    y = None
    for j in range(FF_SPLIT):
        cols = slice(j * ffw, (j + 1) * ffw)
        gate = _dot(h, win_ref[:, cols])
        ext = jnp.concatenate([jnp.where(keep, gate[0:HALO], 0.0), gate[HALO:]], axis=0)
        up = _dot(h[HALO:], win_ref[:, D_FF + j * ffw:D_FF + (j + 1) * ffw])
        conv = (ext[HALO:] * cw[2:3, cols] + _shifted(ext, 1) * cw[1:2, cols]
                + _shifted(ext, 2) * cw[0:1, cols] + cb[:, cols])
        part = _dot(_gelu_tanh(conv) * up, wout_ref[cols, :])
        y = part if y is None else y + part
    o_ref[...] = x1[HALO:] + _rms(y, gpost_ref[...])


def _mix_ffn(x, ya, yb, w_o_all, mix_layer, g_mix_all, g_pre_all, w_in_all, conv_w_all, conv_b_all, w_out_all,
             g_post_all, layer, seq_len, tm=512):
    n, d = x.shape
    return pl.pallas_call(
        functools.partial(_mix_ffn_body, tiles_per_batch=seq_len // tm),
        grid=(n // tm,),
        in_specs=[_row_spec(tm, d), _halo_spec(tm, d),
                  _row_spec(tm, HALF_W), _halo_spec(tm, HALF_W), _row_spec(tm, HALF_W), _halo_spec(tm, HALF_W),
                  _layer_spec(w_o_all.shape, mix_layer), _layer_spec(g_mix_all.shape, layer),
                  _layer_spec(g_pre_all.shape, layer), _layer_spec(w_in_all.shape, layer),
                  _layer_spec(conv_w_all.shape, layer), _layer_spec(conv_b_all.shape, layer),
                  _layer_spec(w_out_all.shape, layer), _layer_spec(g_post_all.shape, layer)],
        out_specs=_row_spec(tm, d),
        out_shape=jax.ShapeDtypeStruct((n, d), F32),
        compiler_params=_params(("parallel",)),
        name="mix_ffn",
    )(x, x, ya, ya, yb, yb, w_o_all, g_mix_all, g_pre_all, w_in_all, conv_w_all, conv_b_all, w_out_all, g_post_all)


def _rows(a):
    return a[:, None, :]


def _pad_rows(w, start):
    return jnp.pad(w, ((0, 0), (start, A_LR_W - start - w.shape[1]), (0, 0)))


def _even_in_weights(ev_w_in, rwkv_vres_down):
    a_cols = 3 * HALF_W + A_DECAY_R + A_AAA_R + A_GATE_R
    n_even, d, _ = ev_w_in.shape
    w = ev_w_in.astype(BF16)
    vres = jnp.concatenate([jnp.zeros((1, d, A_MV_R), BF16), rwkv_vres_down.astype(BF16)], axis=0)
    pad = jnp.zeros((n_even, d, A_LR_W - (a_cols - 3 * HALF_W) - A_MV_R), BF16)
    return jnp.concatenate([w[:, :, 0:a_cols], vres, pad, w[:, :, a_cols:]], axis=2)


def _even_params(rwkv_mu, rwkv_w0, rwkv_w_up, rwkv_a0, rwkv_a_up, rwkv_g_up, rwkv_k_k, rwkv_k_a, rwkv_r_k, rwkv_ln_g,
                 rwkv_ln_b, rwkv_vres_up, rwkv_v0, sgu_ln_g, sgu_ln_b, sgu_w, sgu_b):
    lr_w = A_DECAY_R + A_AAA_R + A_GATE_R
    o_a, o_g = A_DECAY_R, A_DECAY_R + A_AAA_R
    return {
        "mu": _rows(rwkv_mu[:, 0:3 * HALF_W]),
        "mul": _rows(jnp.pad(rwkv_mu[:, 3 * HALF_W:], ((0, 0), (0, A_LR_W - lr_w)))),
        "w0": _rows(rwkv_w0),
        "wup": _pad_rows(rwkv_w_up, 0),
        "a0": _rows(rwkv_a0),
        "aup": _pad_rows(rwkv_a_up, o_a),
        "gup": _pad_rows(rwkv_g_up, o_g),
        "k_k": _rows(rwkv_k_k),
        "k_a": _rows(rwkv_k_a),
        "v0": _rows(rwkv_v0),
        "vup": _pad_rows(rwkv_vres_up, lr_w),
        "ln_g": _rows(rwkv_ln_g),
        "ln_b": _rows(rwkv_ln_b),
        "r_k": rwkv_r_k.reshape(rwkv_r_k.shape[0], 1, HALF_W),
        "sgu_ln_g": _rows(sgu_ln_g),
        "sgu_ln_b": _rows(sgu_ln_b),
        "sgu_w": jnp.concatenate([sgu_w[:, 0::2], sgu_w[:, 1::2]], axis=3),
        "sgu_bias": jnp.repeat(jnp.swapaxes(sgu_b, 1, 2), HALF_W // B_GROUPS, axis=2),
    }


def _odd_in_weights(od_w_in):
    return jnp.pad(od_w_in.astype(BF16), ((0, 0), (0, 0), (0, LANES - 2 * D_HEADS)))


def _head_expand(offset):
    r = jnp.arange(LANES)[:, None]
    c = jnp.arange(HALF_W)[None, :]
    return (r == offset + c // D_HD).astype(BF16)


def kernel(x, norm_mix_pre, norm_mix_post, norm_ffn_pre, norm_ffn_post, ev_w_in, ev_w_out, rwkv_mu, rwkv_w0, rwkv_w_up, rwkv_a0, rwkv_a_up, rwkv_g_up, rwkv_k_k, rwkv_k_a, rwkv_r_k, rwkv_ln_g, rwkv_ln_b, rwkv_vres_down, rwkv_vres_up, rwkv_v0, sgu_ln_g, sgu_ln_b, sgu_w, sgu_b, od_w_in, od_w_out, hgrn_lb_logits, hgrn_norm_g, gdn_conv_w, gdn_a_log, gdn_dt_bias, gdn_norm_g, ffn_w_in, ffn_conv_w, ffn_conv_b, ffn_w_out):
    batch, seq_len, d = x.shape
    depth = norm_mix_pre.shape[0]
    n = batch * seq_len
    xf = x.reshape(n, d)
    ev_w_in_p = _even_in_weights(ev_w_in, rwkv_vres_down)
    od_w_in_p = _odd_in_weights(od_w_in)
    ev_w_out_b = ev_w_out.astype(BF16)
    od_w_out_b = od_w_out.astype(BF16)
    ffn_w_in_b = ffn_w_in.astype(BF16)
    ffn_w_out_b = ffn_w_out.astype(BF16)
    prm = _even_params(rwkv_mu, rwkv_w0, rwkv_w_up, rwkv_a0, rwkv_a_up, rwkv_g_up, rwkv_k_k, rwkv_k_a, rwkv_r_k,
                       rwkv_ln_g, rwkv_ln_b, rwkv_vres_up, rwkv_v0, sgu_ln_g, sgu_ln_b, sgu_w, sgu_b)
    g_mix_pre, g_mix_post = _rows(norm_mix_pre), _rows(norm_mix_post)
    g_ffn_pre, g_ffn_post = _rows(norm_ffn_pre), _rows(norm_ffn_post)
    ffn_conv_b_r = _rows(ffn_conv_b)
    hgrn_g = _rows(hgrn_norm_g)
    gdn_a_log_r = _rows(jnp.repeat(gdn_a_log, D_HD, axis=1))
    gdn_dt_bias_r = _rows(jnp.repeat(gdn_dt_bias, D_HD, axis=1))
    gdn_g = _rows(gdn_norm_g)
    e_b, e_a = _head_expand(0), _head_expand(D_HEADS)
    v_first = None
    for l in range(depth):
        if l % 2 == 0:
            e = l // 2
            z_rkv, z_lr, z_b = _inproj(xf, g_mix_pre, l, ev_w_in_p, e, (3 * HALF_W, A_LR_W, 2 * HALF_W))
            ya, v_a = _rwkv(z_rkv, z_lr, v_first if e > 0 else None, prm, e, batch, seq_len)
            if e == 0:
                v_first = v_a
            yb = _sgu(z_b, prm, e)
            w_out_all, mix_layer = ev_w_out_b, e
        else:
            o = l // 2
            z_c, z_qkv, z_z, z_ba = _inproj(xf, g_mix_pre, l, od_w_in_p, o, (4 * HALF_W, 3 * HALF_W, HALF_W, LANES))
            ya = _hgrn(z_c, hgrn_lb_logits, hgrn_g, o, batch, seq_len)
            yb = _gdn(z_qkv, z_ba, z_z, gdn_conv_w, gdn_a_log_r, gdn_dt_bias_r, e_b, e_a, gdn_g, o, batch, seq_len)
            w_out_all, mix_layer = od_w_out_b, o
        xf = _mix_ffn(xf, ya, yb, w_out_all, mix_layer, g_mix_post, g_ffn_pre, ffn_w_in_b, ffn_conv_w, ffn_conv_b_r,
                      ffn_w_out_b, g_ffn_post, l, seq_len)
    return xf.reshape(batch, seq_len, d)
```

```python
import functools

import jax
import jax.numpy as jnp
from jax import lax
from jax.experimental import pallas as pl
from jax.experimental.pallas import tpu as pltpu

F32 = jnp.float32
BF16 = jnp.bfloat16

D_MODEL = 1024
CHUNK = 64
HALF_W = 512
A_DECAY_R = 32
A_AAA_R = 32
A_MV_R = 32
A_GATE_R = 96
A_LR_W = 256
SGU_BLOCK = 128
B_GROUPS = 8
D_HEADS = 4
D_HD = 128
D_CONV = 4
D_FF = 2816
NORM_EPS = 1e-6
RWKV_LN_EPS = 64e-5
SGU_LN_EPS = 1e-5
L2_EPS = 1e-6

ROWS = 128
HALO = 16
FF_SPLIT = 1
LANES = 128
VMEM_LIMIT = 56 * 1024 * 1024


def _dot(a, b):
    return jnp.dot(a.astype(BF16), b.astype(BF16), preferred_element_type=F32)


def _dot_nt(a, b):
    return lax.dot_general(a.astype(BF16), b.astype(BF16), (((1,), (1,)), ((), ())), preferred_element_type=F32)


def _dot_tn(a, b):
    return lax.dot_general(a.astype(BF16), b.astype(BF16), (((0,), (0,)), ((), ())), preferred_element_type=F32)


def _dot_exact_lhs(m_bf16, x):
    hi = x.astype(BF16)
    lo = (x - hi.astype(F32)).astype(BF16)
    return _dot(m_bf16, hi) + _dot(m_bf16, lo)


def _dot_exact_rhs(x, m_bf16):
    hi = x.astype(BF16)
    lo = (x - hi.astype(F32)).astype(BF16)
    return _dot(hi, m_bf16) + _dot(lo, m_bf16)


def _sigmoid(x):
    return 1.0 / (1.0 + jnp.exp(-x))


def _softplus(x):
    return jnp.maximum(x, 0.0) + jnp.log(1.0 + jnp.exp(-jnp.abs(x)))


def _rms(x, g):
    return x * lax.rsqrt(jnp.mean(x * x, axis=-1, keepdims=True) + NORM_EPS) * g


def _iota(shape, dim):
    return lax.broadcasted_iota(jnp.int32, shape, dim)


def _chunk_tri(n, strict=False):
    r = _iota((n, n), 0)
    c = _iota((n, n), 1)
    same = (r // CHUNK) == (c // CHUNK)
    return same & ((c < r) if strict else (c <= r))


def _seg_ones(width, seg):
    r = _iota((width, width), 0)
    c = _iota((width, width), 1)
    return jnp.where((r // seg) == (c // seg), 1.0, 0.0).astype(BF16)


def _stack_heads(x):
    lane = _iota((CHUNK, LANES), 1)
    m1 = lane < 64
    top, bot = x[0:CHUNK], x[CHUNK:2 * CHUNK]
    return jnp.concatenate([jnp.where(m1, top, 0.0), jnp.where(m1, 0.0, top),
                            jnp.where(m1, bot, 0.0), jnp.where(m1, 0.0, bot)], axis=0)


def _neumann_steps(n_mats, eye, to_bd, out):
    ps = [eye + n for n in n_mats]
    ms = list(n_mats)
    bds = [to_bd(m) for m in ms]
    for _ in range(5):
        ms = [_dot(m, bd) for m, bd in zip(ms, bds)]
        yield
        bds = [to_bd(m) for m in ms]
        ps = [p + _dot(p, bd) for p, bd in zip(ps, bds)]
        yield
    out.extend(ps)


def _round_robin(*gens):
    gens = list(gens)
    while gens:
        for g in list(gens):
            try:
                next(g)
            except StopIteration:
                gens.remove(g)


def _chunk_rows(x, r0, r1):
    w = x.shape[1]
    return jnp.concatenate([jnp.broadcast_to(x[r0:r0 + 1], (CHUNK, w)),
                            jnp.broadcast_to(x[r1:r1 + 1], (CHUNK, w))], axis=0)


def _with_halo(x, halo, keep):
    return jnp.concatenate([jnp.where(keep, halo, 0.0), x], axis=0)


def _shifted(ext, s):
    if s == 0:
        return ext[HALO:]
    return pltpu.roll(ext, s, 0)[HALO:]


def _params(sem):
    return pltpu.CompilerParams(dimension_semantics=sem, vmem_limit_bytes=VMEM_LIMIT)


def _layer_spec(shape, layer):
    tail = (0,) * (len(shape) - 1)
    return pl.BlockSpec((None,) + tuple(shape[1:]), lambda *_: (layer,) + tail, pipeline_mode=pl.Buffered(1))


def _row_spec(tm, width, col=0):
    return pl.BlockSpec((tm, width), lambda i, col=col: (i, col))


def _halo_spec(tm, width, col=0):
    step = tm // HALO
    return pl.BlockSpec((HALO, width), lambda i, col=col: (jnp.maximum(i * step - 1, 0), col))


def _const_spec(shape):
    return pl.BlockSpec(shape, lambda *_: (0,) * len(shape), pipeline_mode=pl.Buffered(1))


def _inproj_body(x_ref, g_ref, w_ref, *o_refs, widths):
    h = _rms(x_ref[...], g_ref[...]).astype(BF16)
    off = 0
    for o_ref, wd in zip(o_refs, widths):
        o_ref[...] = _dot(h, w_ref[:, off:off + wd])
        off += wd


def _inproj(x, g_all, g_layer, w_all, layer, widths, tm=512):
    n, d = x.shape
    return pl.pallas_call(
        functools.partial(_inproj_body, widths=widths),
        grid=(n // tm,),
        in_specs=[_row_spec(tm, d), _layer_spec(g_all.shape, g_layer), _layer_spec(w_all.shape, layer)],
        out_specs=[_row_spec(tm, wd) for wd in widths],
        out_shape=[jax.ShapeDtypeStruct((n, wd), F32) for wd in widths],
        compiler_params=_params(("parallel",)),
        name="inproj",
    )(x, g_all, w_all)


def _rwkv_body(*refs, has_vres, nb):
    if has_vres:
        (z_ref, zh_ref, l_ref, lh_ref, vf_ref, mu_ref, mul_ref, w0_ref, wup_ref, a0_ref, aup_ref, gup_ref,
         kk_ref, ka_ref, v0_ref, vup_ref, lng_ref, lnb_ref, rk_ref, y_ref, s_ref) = refs
    else:
        (z_ref, zh_ref, l_ref, lh_ref, mu_ref, mul_ref, w0_ref, wup_ref, a0_ref, aup_ref, gup_ref,
         kk_ref, ka_ref, lng_ref, lnb_ref, rk_ref, y_ref, vout_ref, s_ref) = refs

    @pl.when(pl.program_id(1) == 0)
    def _():
        s_ref[...] = jnp.zeros_like(s_ref)

    keep = pl.program_id(1) != 0
    z = z_ref[...]
    za_all = z + mu_ref[...] * (_shifted(_with_halo(z, zh_ref[...], keep), 1) - z)
    zl = l_ref[...]
    zl_all = zl + mul_ref[...] * (_shifted(_with_halo(zl, lh_ref[...], keep), 1) - zl)
    seg = _seg_ones(HALF_W, 64)

    pairs = range(HALF_W // LANES)
    sls = [slice(p * LANES, (p + 1) * LANES) for p in pairs]
    tri = jnp.where(_chunk_tri(ROWS), 1.0, 0.0).astype(BF16)
    rr = _iota((ROWS, 2 * ROWS), 0)
    cc = _iota((ROWS, 2 * ROWS), 1)
    same_chunk = (rr // CHUNK) == (cc // ROWS)
    rc_strict = same_chunk & ((cc % CHUNK) < (rr % CHUNK))
    rc_incl = same_chunk & ((cc % CHUNK) <= (rr % CHUNK))

    head_blk = (_iota((LANES, LANES), 0) // CHUNK) == (_iota((LANES, LANES), 1) // CHUNK)
    eye_c = jnp.where((_iota((CHUNK, LANES), 1) % CHUNK) == _iota((CHUNK, LANES), 0), 1.0, 0.0)

    def to_bd_c(m_c):
        return jnp.where(head_blk, jnp.concatenate([m_c, m_c], axis=0), 0.0).astype(BF16)
    q, v_s, a_in_s, r_in, bk_t, e_col, post_in = {}, {}, {}, {}, {}, {}, {}
    a_rb, av, tt, r_eff, y_off, mb = {}, {}, {}, {}, {}, {}
    zero_blk = jnp.zeros((LANES, LANES), BF16)

    def prepare(blk):
        rows = slice(blk * ROWS, (blk + 1) * ROWS)
        za = za_all[rows]
        zl = zl_all[rows]
        r = za[:, 0:HALF_W]
        k_raw = za[:, HALF_W:2 * HALF_W]
        v = za[:, 2 * HALF_W:3 * HALF_W]
        if has_vres:
            gate = _sigmoid(v0_ref[...] + _dot(zl, vup_ref[...]))
            v = v + (vf_ref[rows, :] - v) * gate
        else:
            vout_ref[rows, :] = v
        yield
        w_log = -_softplus(-(w0_ref[...] + _dot(jnp.tanh(zl), wup_ref[...]))) - 0.5
        lw = -jnp.exp(w_log)
        a = _sigmoid(a0_ref[...] + _dot(zl, aup_ref[...]))
        g = _dot(_sigmoid(zl), gup_ref[...])
        yield
        kkk = k_raw * kk_ref[...]
        kk = kkk * lax.rsqrt(_dot(kkk * kkk, seg) + L2_EPS)
        k = k_raw * (1.0 + (a - 1.0) * ka_ref[...])
        bonus = _dot_exact_rhs(r * k * rk_ref[...], seg)
        post_in[blk] = (bonus * v, g)
        yield
        b = _dot_exact_lhs(tri, lw)
        bref = _chunk_rows(b, CHUNK // 2, CHUNK + CHUNK // 2)
        blast = _chunk_rows(b, CHUNK - 1, 2 * CHUNK - 1)
        e_pos = jnp.exp(b - bref)
        e_neg = jnp.exp(bref - b)
        e_in = jnp.exp(b)
        e_st = jnp.exp(blast - b)
        e_last = jnp.exp(blast)
        yield
        kka = kk * a
        neg_kk_exc = -(kk * jnp.exp(-lw))
        a_t = neg_kk_exc * e_pos
        a_in = neg_kk_exc * e_in
        r_t = r * e_pos
        r_in_blk = r * e_in
        b_t = kka * e_neg
        b_st = kka * e_st
        k_t = k * e_neg
        k_st = k * e_st
        for p in pairs:
            sl = sls[p]
            un = (blk, p)
            q[un] = _dot_nt(jnp.concatenate([a_t[:, sl], r_t[:, sl]], axis=0),
                            jnp.concatenate([_stack_heads(b_t[:, sl]), _stack_heads(k_t[:, sl])], axis=0))
            v_s[un] = _stack_heads(v[:, sl]).astype(BF16)
            a_in_s[un] = _stack_heads(a_in[:, sl])
            r_in[un] = r_in_blk[:, sl]
            b_s = _stack_heads(b_st[:, sl])
            k_s = _stack_heads(k_st[:, sl])
            for c in range(2):
                ss = slice(c * LANES, (c + 1) * LANES)
                bk_t[un + (c,)] = jnp.concatenate([b_s[ss].T, k_s[ss].T], axis=1).astype(BF16)
                e_col[un + (c,)] = jnp.broadcast_to(e_last[c * CHUNK:c * CHUNK + 1, sl], (LANES, LANES)).T
            yield

    def solve(units):
        a_kk = {}
        for u in units:
            a_rb[u] = jnp.where(rc_incl, q[u][ROWS:2 * ROWS, 0:256], 0.0).astype(BF16)
            a_kk[u] = jnp.concatenate([jnp.where(rc_strict, q[u][0:ROWS, 256:512], 0.0),
                                       jnp.where(rc_incl, q[u][ROWS:2 * ROWS, 256:512], 0.0)], axis=0)
        yield
        n_list = [jnp.where(rc_strict, q[u][0:ROWS, 0:256], 0.0)[c * CHUNK:(c + 1) * CHUNK, c * LANES:(c + 1) * LANES]
                  for u in units for c in range(2)]
        t_list = []
        yield from _neumann_steps(n_list, eye_c, to_bd_c, t_list)
        for u in units:
            av[u] = _dot(a_kk[u], v_s[u])
        yield
        for i, u in enumerate(units):
            x_s = jnp.concatenate([a_in_s[u], _stack_heads(av[u][0:ROWS])], axis=1).astype(BF16)
            tt[u] = jnp.concatenate([_dot(t_list[2 * i + c], x_s[c * LANES:(c + 1) * LANES]) for c in range(2)],
                                    axis=0)
        yield
        wu_s = {u: jnp.concatenate([_stack_heads(tt[u][:, 0:LANES]), _stack_heads(tt[u][:, LANES:2 * LANES])],
                                   axis=1).astype(BF16) for u in units}
        for u in units:
            ru = _dot(a_rb[u], wu_s[u])
            r_eff[u] = (r_in[u] + ru[:, 0:LANES]).astype(BF16)
            y_off[u] = ru[:, LANES:2 * LANES] + av[u][ROWS:2 * ROWS]
        yield
        for c in range(2):
            ss = slice(c * LANES, (c + 1) * LANES)
            for u in units:
                rhs = jnp.concatenate([wu_s[u][ss], jnp.concatenate([zero_blk, v_s[u][ss]], axis=1)], axis=0)
                mb[u + (c,)] = _dot(bk_t[u + (c,)], rhs)
            yield

    state = [s_ref[p] for p in pairs]

    def sequential(blk):
        rows = slice(blk * ROWS, (blk + 1) * ROWS)
        for c in range(2):
            rs = slice(c * CHUNK, (c + 1) * CHUNK)
            out_rows = slice(blk * ROWS + c * CHUNK, blk * ROWS + (c + 1) * CHUNK)
            for p in pairs:
                un = (blk, p)
                h_b = state[p].astype(BF16)
                y_ref[out_rows, sls[p]] = _dot(r_eff[un][rs], h_b) + y_off[un][rs]
                m_b = mb[un + (c,)]
                state[p] = state[p] * e_col[un + (c,)] + _dot(m_b[:, 0:LANES], h_b) + m_b[:, LANES:2 * LANES]
            yield
        y = y_ref[rows, :]
        mu_y = _dot_exact_rhs(y, seg) * (1.0 / 64)
        d = y - mu_y
        var = _dot(d * d, seg) * (1.0 / 64)
        bonus_v, g = post_in[blk]
        y_ref[rows, :] = (d * lax.rsqrt(var + RWKV_LN_EPS) * lng_ref[...] + lnb_ref[...] + bonus_v) * g
        yield

    first, second = list(range(nb // 2)), list(range(nb // 2, nb))

    def each(fn, blks):
        for blk in blks:
            yield from fn(blk)

    _round_robin(each(prepare, first))
    _round_robin(solve([(blk, p) for blk in first for p in pairs]), each(prepare, second))
    _round_robin(solve([(blk, p) for blk in second for p in pairs]), each(sequential, first))
    _round_robin(each(sequential, second))
    for p in pairs:
        s_ref[p] = state[p]


def _rwkv(z_rkv, z_lr, v_first, prm, e, batch, seq_len, nb=4):
    n = z_rkv.shape[0]
    has_vres = v_first is not None
    rows = nb * ROWS
    nblk = seq_len // rows
    w3 = 3 * HALF_W

    def tile(width):
        return pl.BlockSpec((rows, width), lambda b, j: (b * nblk + j, 0))

    def halo(width):
        step = rows // HALO
        return pl.BlockSpec((HALO, width), lambda b, j: (jnp.maximum((b * nblk + j) * step - 1, 0), 0))

    ins = [z_rkv, z_rkv, z_lr, z_lr]
    specs = [tile(w3), halo(w3), tile(A_LR_W), halo(A_LR_W)]
    if has_vres:
        ins.append(v_first)
        specs.append(tile(HALF_W))
    names = (["mu", "mul", "w0", "wup", "a0", "aup", "gup", "k_k", "k_a"] + (["v0", "vup"] if has_vres else [])
             + ["ln_g", "ln_b", "r_k"])
    for nm in names:
        ins.append(prm[nm])
        specs.append(_layer_spec(prm[nm].shape, e - 1 if nm in ("v0", "vup") else e))
    n_out = 1 if has_vres else 2
    out = pl.pallas_call(
        functools.partial(_rwkv_body, has_vres=has_vres, nb=nb),
        grid=(batch, nblk),
        in_specs=specs,
        out_specs=[tile(HALF_W)] * n_out,
        out_shape=[jax.ShapeDtypeStruct((n, HALF_W), F32)] * n_out,
        scratch_shapes=[pltpu.VMEM((HALF_W // LANES, LANES, LANES), F32)],
        compiler_params=_params(("parallel", "arbitrary")),
        name="rwkv",
    )(*ins)
    return (out[0], None) if has_vres else (out[0], out[1])


def _gelu_erf(x):
    return 0.5 * x * (1.0 + lax.erf(x * (2.0 ** -0.5)))


def _sgu_body(u_ref, v_ref, lng_ref, lnb_ref, w_ref, bias_ref, o_ref, *, tm):
    seg = _seg_ones(HALF_W, 64)
    u = _gelu_erf(u_ref[...])
    v = _gelu_erf(v_ref[...])
    mu = _dot_exact_rhs(v, seg) * (1.0 / 64)
    d = v - mu
    var = _dot(d * d, seg) * (1.0 / 64)
    vn = d * lax.rsqrt(var + SGU_LN_EPS) * lng_ref[...] + lnb_ref[...]
    r = _iota((SGU_BLOCK, 2 * SGU_BLOCK), 0)
    c = _iota((SGU_BLOCK, 2 * SGU_BLOCK), 1)
    causal = (r // CHUNK) >= ((c % SGU_BLOCK) // CHUNK)
    lane = _iota((SGU_BLOCK, LANES), 1)
    m1 = lane < 64
    bias = bias_ref[...]
    for p in range(HALF_W // LANES):
        sl = slice(p * LANES, (p + 1) * LANES)
        w = jnp.where(causal, w_ref[p], 0.0)
        for nb in range(tm // SGU_BLOCK):
            rs = slice(nb * SGU_BLOCK, (nb + 1) * SGU_BLOCK)
            blk = vn[rs, sl]
            stacked = jnp.concatenate([jnp.where(m1, blk, 0.0), jnp.where(m1, 0.0, blk)], axis=0)
            mixed = _dot(w, stacked) + bias[:, sl]
            o_ref[rs, sl] = u[rs, sl] * mixed


def _sgu(z_b, prm, e, tm=512):
    n = z_b.shape[0]
    names = ["sgu_ln_g", "sgu_ln_b", "sgu_w", "sgu_bias"]
    return pl.pallas_call(
        functools.partial(_sgu_body, tm=tm),
        grid=(n // tm,),
        in_specs=[_row_spec(tm, HALF_W, 0), _row_spec(tm, HALF_W, 1)]
        + [_layer_spec(prm[nm].shape, e) for nm in names],
        out_specs=_row_spec(tm, HALF_W),
        out_shape=jax.ShapeDtypeStruct((n, HALF_W), F32),
        compiler_params=_params(("parallel",)),
        name="sgu",
    )(z_b, z_b, *[prm[nm] for nm in names])


def _hgrn_body(q_ref, f_ref, i_ref, gt_ref, lbl_ref, ng_ref, y_ref, s_ref, *, layer, nb):
    @pl.when(pl.program_id(1) == 0)
    def _():
        s_ref[...] = jnp.zeros_like(s_ref)

    logits = lbl_ref[...]
    e = jnp.exp(logits - jnp.max(logits, axis=0, keepdims=True))
    prob = e / jnp.sum(e, axis=0, keepdims=True)
    lb = jnp.sum(prob[0:layer + 1], axis=0, keepdims=True) - prob[0:1]

    f = lb + (1.0 - lb) * _sigmoid(f_ref[...])
    lf_all = jnp.log(f)
    kx_all = 1.0 - f
    q = q_ref[...]
    qs_all = q * _sigmoid(q)
    tri = jnp.where(_chunk_tri(ROWS), 1.0, 0.0).astype(BF16)
    rr = _iota((2 * ROWS, ROWS), 0)
    cc = _iota((2 * ROWS, ROWS), 1)
    valid = ((rr // ROWS) == (cc // CHUNK)) & ((cc % CHUNK) <= (rr % CHUNK))
    m1 = _iota((CHUNK, LANES), 1) < 64
    blockmask = (_iota((LANES, LANES), 0) // 64) == (_iota((LANES, LANES), 1) // 64)
    pairs = range(HALF_W // LANES)
    state = [s_ref[p] for p in pairs]

    for blk in range(nb):
        rows = slice(blk * ROWS, (blk + 1) * ROWS)
        qs = qs_all[rows]
        kx = kx_all[rows]
        v = i_ref[rows, :]
        b = _dot_exact_lhs(tri, lf_all[rows])
        bref = _chunk_rows(b, CHUNK // 2, CHUNK + CHUNK // 2)
        blast = _chunk_rows(b, CHUNK - 1, 2 * CHUNK - 1)
        q_t = qs * jnp.exp(b - bref)
        k_t = kx * jnp.exp(bref - b)
        q_in = qs * jnp.exp(b)
        k_st = kx * jnp.exp(blast - b)
        e_last = jnp.exp(blast)
        for p in pairs:
            sl = slice(p * LANES, (p + 1) * LANES)
            sc = _dot_nt(_stack_heads(q_t[:, sl]), k_t[:, sl])
            oi = _dot(jnp.where(valid, sc, 0.0), v[:, sl])
            for c in range(2):
                rs = slice(c * CHUNK, (c + 1) * CHUNK)
                o_inter = _dot_nt(q_in[rs, sl], state[p])
                o_intra = (jnp.where(m1, oi[c * 128:c * 128 + 64], 0.0)
                           + jnp.where(m1, 0.0, oi[c * 128 + 64:c * 128 + 128]))
                y_ref[blk * ROWS + c * CHUNK:blk * ROWS + (c + 1) * CHUNK, sl] = o_inter + o_intra
                upd = jnp.where(blockmask, _dot_tn(v[rs, sl], k_st[rs, sl]), 0.0)
                state[p] = state[p] * e_last[c * CHUNK:c * CHUNK + 1, sl] + upd
    for p in pairs:
        s_ref[p] = state[p]

    o = y_ref[...]
    ms = _dot(o * o, _seg_ones(HALF_W, 64)) * (1.0 / 64)
    gate = gt_ref[...]
    y_ref[...] = o * lax.rsqrt(ms + NORM_EPS) * ng_ref[...] * (gate * _sigmoid(gate))


def _hgrn(z_c, lb_logits, norm_g_all, layer, batch, seq_len, nb=8):
    n = z_c.shape[0]
    rows = nb * ROWS
    nblk = seq_len // rows

    def spec(col):
        return pl.BlockSpec((rows, HALF_W), lambda b, j: (b * nblk + j, col))

    return pl.pallas_call(
        functools.partial(_hgrn_body, layer=layer, nb=nb),
        grid=(batch, nblk),
        in_specs=[spec(c) for c in range(4)] + [_const_spec(lb_logits.shape), _layer_spec(norm_g_all.shape, layer)],
        out_specs=spec(0),
        out_shape=jax.ShapeDtypeStruct((n, HALF_W), F32),
        scratch_shapes=[pltpu.VMEM((HALF_W // LANES, LANES, LANES), F32)],
        compiler_params=_params(("parallel", "arbitrary")),
        name="hgrn",
    )(z_c, z_c, z_c, z_c, lb_logits, norm_g_all)


def _gdn_body(z_ref, zh_ref, ba_ref, zz_ref, cw_ref, alog_ref, dtb_ref, eb_ref, ea_ref, ng_ref, y_ref, s_ref, *, nb):
    @pl.when(pl.program_id(1) == 0)
    def _():
        s_ref[...] = jnp.zeros_like(s_ref)

    keep = pl.program_id(1) != 0
    ext = _with_halo(z_ref[...], zh_ref[...], keep)
    cw = cw_ref[...]
    seg = _seg_ones(HALF_W, D_HD)
    tri = jnp.where(_chunk_tri(ROWS), 1.0, 0.0).astype(BF16)
    incl = _chunk_tri(ROWS)
    strict = _chunk_tri(ROWS, strict=True)
    heads = range(D_HEADS)
    sls = [slice(h * D_HD, (h + 1) * D_HD) for h in heads]
    eye = jnp.where(_iota((ROWS, ROWS), 0) == _iota((ROWS, ROWS), 1), 1.0, 0.0)
    eye2 = jnp.concatenate([eye, eye], axis=1)
    zero_blk = jnp.zeros((ROWS, D_HD), BF16)
    chunk_row = _iota((ROWS, 2 * D_HD), 0) // CHUNK

    def to_bd(m_rc):
        mb = m_rc.astype(BF16)
        return jnp.concatenate([jnp.concatenate([mb[:, 0:D_HD], zero_blk], axis=1),
                                jnp.concatenate([zero_blk, mb[:, D_HD:2 * D_HD]], axis=1)], axis=0)

    neg_lower, qk, rhs, wq, k_st_t, d_last = {}, {}, {}, {}, {}, {}
    sol, qo, q_eff, kuw = {}, {}, {}, {}

    def prepare(blk):
        rows = slice(blk * ROWS, (blk + 1) * ROWS)
        win = ext[blk * ROWS:blk * ROWS + ROWS + HALO]
        acc = win[HALO:] * cw[D_CONV - 1:D_CONV]
        for j in range(D_CONV - 1):
            acc = acc + _shifted(win, D_CONV - 1 - j) * cw[j:j + 1]
        yield
        qkv = acc * _sigmoid(acc)
        q_raw = qkv[:, 0:HALF_W]
        k_raw = qkv[:, HALF_W:2 * HALF_W]
        v_all = qkv[:, 2 * HALF_W:3 * HALF_W]
        q_all = q_raw * lax.rsqrt(_dot(q_raw * q_raw, seg) + L2_EPS) * (D_HD ** -0.5)
        k_all = k_raw * lax.rsqrt(_dot(k_raw * k_raw, seg) + L2_EPS)
        yield
        ba = ba_ref[rows, :]
        beta_all = _sigmoid(_dot_exact_rhs(ba, eb_ref[...]))
        g_all = -jnp.exp(alog_ref[...]) * _softplus(_dot_exact_rhs(ba, ea_ref[...]) + dtb_ref[...])
        gc_all = _dot_exact_lhs(tri, g_all)
        glast_all = _chunk_rows(gc_all, CHUNK - 1, 2 * CHUNK - 1)
        yield
        for h in heads:
            un = (blk, h)
            gc = gc_all[:, sls[h]]
            glast = glast_all[:, sls[h]]
            diff = gc - gc.T
            decay = jnp.where(incl, jnp.exp(jnp.where(incl, diff, 0.0)), 0.0)
            q = q_all[:, sls[h]]
            k = k_all[:, sls[h]]
            beta = beta_all[:, sls[h]]
            kb = k * beta
            kq = _dot_nt(jnp.concatenate([kb, q], axis=0), k)
            neg_lower[un] = -jnp.where(strict, kq[0:ROWS] * decay, 0.0)
            qk[un] = (kq[ROWS:2 * ROWS] * decay).astype(BF16)
            egc = jnp.exp(gc)
            rhs[un] = jnp.concatenate([v_all[:, sls[h]] * beta, kb * egc], axis=1)
            wq[un] = q * egc
            k_st_t[un] = (k * jnp.exp(glast - gc)).T.astype(BF16)
            d_last[un] = jnp.exp(glast)
            yield

    def solve(blks):
        units = [(blk, h) for blk in blks for h in heads]
        pair_units = [(blk, h) for blk in blks for h in range(0, D_HEADS, 2)]
        t_list = []
        yield from _neumann_steps([jnp.concatenate([neg_lower[(blk, h)], neg_lower[(blk, h + 1)]], axis=1)
                                   for blk, h in pair_units], eye2, to_bd, t_list)
        t_pairs = dict(zip(pair_units, t_list))
        for blk, h in units:
            sol[(blk, h)] = _dot(t_pairs[(blk, h - h % 2)][:, (h % 2) * D_HD:(h % 2 + 1) * D_HD], rhs[(blk, h)])
        yield
        for un in units:
            qo[un] = _dot(qk[un], sol[un])
            q_eff[un] = (wq[un] - qo[un][:, D_HD:2 * D_HD]).astype(BF16)
        yield
        for c in range(2):
            for un in units:
                kuw[un + (c,)] = _dot(k_st_t[un], jnp.where(chunk_row == c, sol[un], 0.0))
            yield

    state = [s_ref[h] for h in heads]

    def sequential(blk):
        for c in range(2):
            rs = slice(c * CHUNK, (c + 1) * CHUNK)
            out_rows = slice(blk * ROWS + c * CHUNK, blk * ROWS + (c + 1) * CHUNK)
            for h in heads:
                un = (blk, h)
                s_b = state[h].astype(BF16)
                o = _dot(q_eff[un][rs], s_b) + qo[un][rs, 0:D_HD]
                k_uw = kuw[un + (c,)]
                state[h] = (state[h] * d_last[un][c * CHUNK:c * CHUNK + 1] + k_uw[:, 0:D_HD]
                            - _dot(k_uw[:, D_HD:2 * D_HD], s_b))
                z = zz_ref[out_rows, sls[h]]
                y_ref[out_rows, sls[h]] = _rms(o, ng_ref[...]) * (z * _sigmoid(z))
            yield

    first, second = list(range(nb // 2)), list(range(nb // 2, nb))

    def each(fn, blks):
        for blk in blks:
            yield from fn(blk)

    _round_robin(each(prepare, first))
    _round_robin(solve(first), each(prepare, second))
    _round_robin(solve(second), each(sequential, first))
    _round_robin(each(sequential, second))
    for h in heads:
        s_ref[h] = state[h]


def _gdn(z_qkv, z_ba, z_z, conv_w_all, a_log_all, dt_bias_all, e_b, e_a, norm_g_all, layer, batch, seq_len, nb=8):
    n = z_qkv.shape[0]
    rows = nb * ROWS
    nblk = seq_len // rows
    w3 = 3 * HALF_W

    def tile(width):
        return pl.BlockSpec((rows, width), lambda b, j: (b * nblk + j, 0))

    step = rows // HALO
    halo = pl.BlockSpec((HALO, w3), lambda b, j: (jnp.maximum((b * nblk + j) * step - 1, 0), 0))
    return pl.pallas_call(
        functools.partial(_gdn_body, nb=nb),
        grid=(batch, nblk),
        in_specs=[tile(w3), halo, tile(LANES), tile(HALF_W),
                  _layer_spec(conv_w_all.shape, layer), _layer_spec(a_log_all.shape, layer),
                  _layer_spec(dt_bias_all.shape, layer),
                  _const_spec((LANES, HALF_W)), _const_spec((LANES, HALF_W)), _layer_spec(norm_g_all.shape, layer)],
        out_specs=tile(HALF_W),
        out_shape=jax.ShapeDtypeStruct((n, HALF_W), F32),
        scratch_shapes=[pltpu.VMEM((D_HEADS, D_HD, D_HD), F32)],
        compiler_params=_params(("parallel", "arbitrary")),
        name="gdn",
    )(z_qkv, z_qkv, z_ba, z_z, conv_w_all, a_log_all, dt_bias_all, e_b, e_a, norm_g_all)


def _gelu_tanh(x):
    return 0.5 * x * (1.0 + jnp.tanh((2.0 / jnp.pi) ** 0.5 * (x + 0.044715 * (x * x * x))))


def _mix_ffn_body(x_ref, xh_ref, ya_ref, yah_ref, yb_ref, ybh_ref, wo_ref, gmix_ref, gpre_ref,
                  win_ref, cw_ref, cb_ref, wout_ref, gpost_ref, o_ref, *, tiles_per_batch):
    keep = (pl.program_id(0) % tiles_per_batch) != 0
    xe = jnp.concatenate([xh_ref[...], x_ref[...]], axis=0)
    yae = jnp.concatenate([yah_ref[...], ya_ref[...]], axis=0)
    ybe = jnp.concatenate([ybh_ref[...], yb_ref[...]], axis=0)
    mix = _dot(yae, wo_ref[0:HALF_W, :]) + _dot(ybe, wo_ref[HALF_W:2 * HALF_W, :])
    x1 = xe + _rms(mix, gmix_ref[...])
    h = _rms(x1, gpre_ref[...]).astype(BF16)
    cw = cw_ref[...]
    cb = cb_ref[...]
    ffw = D_FF // FF_SPLIT
    y = None
    for j in range(FF_SPLIT):
        cols = slice(j * ffw, (j + 1) * ffw)
        gate = _dot(h, win_ref[:, cols])
        ext = jnp.concatenate([jnp.where(keep, gate[0:HALO], 0.0), gate[HALO:]], axis=0)
        up = _dot(h[HALO:], win_ref[:, D_FF + j * ffw:D_FF + (j + 1) * ffw])
        conv = (ext[HALO:] * cw[2:3, cols] + _shifted(ext, 1) * cw[1:2, cols]
                + _shifted(ext, 2) * cw[0:1, cols] + cb[:, cols])
        part = _dot(_gelu_tanh(conv) * up, wout_ref[cols, :])
        y = part if y is None else y + part
    o_ref[...] = x1[HALO:] + _rms(y, gpost_ref[...])


def _mix_ffn(x, ya, yb, w_o_all, mix_layer, g_mix_all, g_pre_all, w_in_all, conv_w_all, conv_b_all, w_out_all,
             g_post_all, layer, seq_len, tm=512):
    n, d = x.shape
    return pl.pallas_call(
        functools.partial(_mix_ffn_body, tiles_per_batch=seq_len // tm),
        grid=(n // tm,),
        in_specs=[_row_spec(tm, d), _halo_spec(tm, d),
                  _row_spec(tm, HALF_W), _halo_spec(tm, HALF_W), _row_spec(tm, HALF_W), _halo_spec(tm, HALF_W),
                  _layer_spec(w_o_all.shape, mix_layer), _layer_spec(g_mix_all.shape, layer),
                  _layer_spec(g_pre_all.shape, layer), _layer_spec(w_in_all.shape, layer),
                  _layer_spec(conv_w_all.shape, layer), _layer_spec(conv_b_all.shape, layer),
                  _layer_spec(w_out_all.shape, layer), _layer_spec(g_post_all.shape, layer)],
        out_specs=_row_spec(tm, d),
        out_shape=jax.ShapeDtypeStruct((n, d), F32),
        compiler_params=_params(("parallel",)),
        name="mix_ffn",
    )(x, x, ya, ya, yb, yb, w_o_all, g_mix_all, g_pre_all, w_in_all, conv_w_all, conv_b_all, w_out_all, g_post_all)


def _rows(a):
    return a[:, None, :]


def _pad_rows(w, start):
    return jnp.pad(w, ((0, 0), (start, A_LR_W - start - w.shape[1]), (0, 0)))


def _even_in_weights(ev_w_in, rwkv_vres_down):
    a_cols = 3 * HALF_W + A_DECAY_R + A_AAA_R + A_GATE_R
    n_even, d, _ = ev_w_in.shape
    w = ev_w_in.astype(BF16)
    vres = jnp.concatenate([jnp.zeros((1, d, A_MV_R), BF16), rwkv_vres_down.astype(BF16)], axis=0)
    pad = jnp.zeros((n_even, d, A_LR_W - (a_cols - 3 * HALF_W) - A_MV_R), BF16)
    return jnp.concatenate([w[:, :, 0:a_cols], vres, pad, w[:, :, a_cols:]], axis=2)


def _even_params(rwkv_mu, rwkv_w0, rwkv_w_up, rwkv_a0, rwkv_a_up, rwkv_g_up, rwkv_k_k, rwkv_k_a, rwkv_r_k, rwkv_ln_g,
                 rwkv_ln_b, rwkv_vres_up, rwkv_v0, sgu_ln_g, sgu_ln_b, sgu_w, sgu_b):
    lr_w = A_DECAY_R + A_AAA_R + A_GATE_R
    o_a, o_g = A_DECAY_R, A_DECAY_R + A_AAA_R
    return {
        "mu": _rows(rwkv_mu[:, 0:3 * HALF_W]),
        "mul": _rows(jnp.pad(rwkv_mu[:, 3 * HALF_W:], ((0, 0), (0, A_LR_W - lr_w)))),
        "w0": _rows(rwkv_w0),
        "wup": _pad_rows(rwkv_w_up, 0),
        "a0": _rows(rwkv_a0),
        "aup": _pad_rows(rwkv_a_up, o_a),
        "gup": _pad_rows(rwkv_g_up, o_g),
        "k_k": _rows(rwkv_k_k),
        "k_a": _rows(rwkv_k_a),
        "v0": _rows(rwkv_v0),
        "vup": _pad_rows(rwkv_vres_up, lr_w),
        "ln_g": _rows(rwkv_ln_g),
        "ln_b": _rows(rwkv_ln_b),
        "r_k": rwkv_r_k.reshape(rwkv_r_k.shape[0], 1, HALF_W),
        "sgu_ln_g": _rows(sgu_ln_g),
        "sgu_ln_b": _rows(sgu_ln_b),
        "sgu_w": jnp.concatenate([sgu_w[:, 0::2], sgu_w[:, 1::2]], axis=3),
        "sgu_bias": jnp.repeat(jnp.swapaxes(sgu_b, 1, 2), HALF_W // B_GROUPS, axis=2),
    }


def _odd_in_weights(od_w_in):
    return jnp.pad(od_w_in.astype(BF16), ((0, 0), (0, 0), (0, LANES - 2 * D_HEADS)))


def _head_expand(offset):
    r = jnp.arange(LANES)[:, None]
    c = jnp.arange(HALF_W)[None, :]
    return (r == offset + c // D_HD).astype(BF16)


def kernel(x, norm_mix_pre, norm_mix_post, norm_ffn_pre, norm_ffn_post, ev_w_in, ev_w_out, rwkv_mu, rwkv_w0, rwkv_w_up, rwkv_a0, rwkv_a_up, rwkv_g_up, rwkv_k_k, rwkv_k_a, rwkv_r_k, rwkv_ln_g, rwkv_ln_b, rwkv_vres_down, rwkv_vres_up, rwkv_v0, sgu_ln_g, sgu_ln_b, sgu_w, sgu_b, od_w_in, od_w_out, hgrn_lb_logits, hgrn_norm_g, gdn_conv_w, gdn_a_log, gdn_dt_bias, gdn_norm_g, ffn_w_in, ffn_conv_w, ffn_conv_b, ffn_w_out):
    batch, seq_len, d = x.shape
    depth = norm_mix_pre.shape[0]
    n = batch * seq_len
    xf = x.reshape(n, d)
    ev_w_in_p = _even_in_weights(ev_w_in, rwkv_vres_down)
    od_w_in_p = _odd_in_weights(od_w_in)
    ev_w_out_b = ev_w_out.astype(BF16)
    od_w_out_b = od_w_out.astype(BF16)
    ffn_w_in_b = ffn_w_in.astype(BF16)
    ffn_w_out_b = ffn_w_out.astype(BF16)
    prm = _even_params(rwkv_mu, rwkv_w0, rwkv_w_up, rwkv_a0, rwkv_a_up, rwkv_g_up, rwkv_k_k, rwkv_k_a, rwkv_r_k,
                       rwkv_ln_g, rwkv_ln_b, rwkv_vres_up, rwkv_v0, sgu_ln_g, sgu_ln_b, sgu_w, sgu_b)
    g_mix_pre, g_mix_post = _rows(norm_mix_pre), _rows(norm_mix_post)
    g_ffn_pre, g_ffn_post = _rows(norm_ffn_pre), _rows(norm_ffn_post)
    ffn_conv_b_r = _rows(ffn_conv_b)
    hgrn_g = _rows(hgrn_norm_g)
    gdn_a_log_r = _rows(jnp.repeat(gdn_a_log, D_HD, axis=1))
    gdn_dt_bias_r = _rows(jnp.repeat(gdn_dt_bias, D_HD, axis=1))
    gdn_g = _rows(gdn_norm_g)
    e_b, e_a = _head_expand(0), _head_expand(D_HEADS)
    v_first = None
    for l in range(depth):
        if l % 2 == 0:
            e = l // 2
            z_rkv, z_lr, z_b = _inproj(xf, g_mix_pre, l, ev_w_in_p, e, (3 * HALF_W, A_LR_W, 2 * HALF_W))
            ya, v_a = _rwkv(z_rkv, z_lr, v_first if e > 0 else None, prm, e, batch, seq_len)
            if e == 0:
                v_first = v_a
            yb = _sgu(z_b, prm, e)
            w_out_all, mix_layer = ev_w_out_b, e
        else:
            o = l // 2
            z_c, z_qkv, z_z, z_ba = _inproj(xf, g_mix_pre, l, od_w_in_p, o, (4 * HALF_W, 3 * HALF_W, HALF_W, LANES))
            ya = _hgrn(z_c, hgrn_lb_logits, hgrn_g, o, batch, seq_len)
            yb = _gdn(z_qkv, z_ba, z_z, gdn_conv_w, gdn_a_log_r, gdn_dt_bias_r, e_b, e_a, gdn_g, o, batch, seq_len)
            w_out_all, mix_layer = od_w_out_b, o
        xf = _mix_ffn(xf, ya, yb, w_out_all, mix_layer, g_mix_post, g_ffn_pre, ffn_w_in_b, ffn_conv_w, ffn_conv_b_r,
                      ffn_w_out_b, g_ffn_post, l, seq_len)
    return xf.reshape(batch, seq_len, d)
```
